```python
import math, functools
import jax, jax.numpy as jnp
from jax import lax
import numpy as np

D_MODEL = 2048
BATCH = 8
SEQ = 2048
DEPTH = 4

GRID_W = 64
CTX_LEN = 256
Q_BLOCK = 128
ROPE_BASE = 10000.0
NORM_EPS = 1e-6
N_MIXERS = 3

DA_HEAD_DIM = 64
DA_HEADS = D_MODEL // (2 * DA_HEAD_DIM)
DA_V_DIM = 2 * DA_HEAD_DIM

GQ_HEAD_DIM = 128
GQ_HEADS = D_MODEL // GQ_HEAD_DIM
GQ_KV_HEADS = GQ_HEADS // 4
GQ_Q_W = GQ_HEADS * GQ_HEAD_DIM
GQ_KV_W = GQ_KV_HEADS * GQ_HEAD_DIM

ML_NOPE_DIM = 128
ML_ROPE_DIM = 64
ML_V_DIM = 128
ML_HEADS = D_MODEL // ML_V_DIM
ML_Q_RANK = D_MODEL // 4
ML_KV_RANK = D_MODEL // 4

kernel_name = 'hybrid_diffattn_gqa_mla_prefix_dit'


def rmsnorm(x, g):
    xf = x.astype(jnp.float32)
    y = xf * lax.rsqrt(jnp.mean(xf * xf, axis=-1, keepdims=True) + NORM_EPS)
    return (y * g.astype(jnp.float32)).astype(x.dtype)


def rope_1d(x, pos):
    half = x.shape[-1] // 2
    inv_freq = ROPE_BASE ** (-jnp.arange(half, dtype=jnp.float32) / half)
    ang = pos[:, None] * inv_freq[None, :]
    cos = jnp.cos(ang)[None, :, None, :]
    sin = jnp.sin(ang)[None, :, None, :]
    xf = x.astype(jnp.float32)
    x1, x2 = xf[..., :half], xf[..., half:]
    return jnp.concatenate([x1 * cos - x2 * sin, x1 * sin + x2 * cos], axis=-1).astype(x.dtype)


def rope_2d(x, rows, cols):
    d = x.shape[-1] // 2
    return jnp.concatenate([rope_1d(x[..., :d], rows), rope_1d(x[..., d:], cols)], axis=-1)


def grid_positions(n_tokens):
    n_rows = n_tokens // GRID_W
    rows = jnp.repeat(jnp.arange(n_rows, dtype=jnp.float32), GRID_W)
    cols = jnp.tile(jnp.arange(GRID_W, dtype=jnp.float32), n_rows)
    return rows, cols


def sweep_query_blocks(fn, qs):
    b, s = qs[0].shape[:2]
    nb = s // Q_BLOCK
    blocks = tuple(jnp.swapaxes(q.reshape((b, nb, Q_BLOCK) + q.shape[2:]), 0, 1) for q in qs)
    out = lax.map(lambda blk: fn(*blk), blocks)
    out = jnp.swapaxes(out, 0, 1)
    return out.reshape((b, s) + out.shape[3:])


def lambda_init_fn(layer):
    return 0.8 - 0.6 * math.exp(-0.3 * layer)


def diff_attn_core(q, k, v, lam):
    s = jnp.einsum('bqhnd,bkhnd->bhnqk', q, k).astype(jnp.float32) * (DA_HEAD_DIM ** -0.5)
    p = jax.nn.softmax(s, axis=-1)
    a = p[:, :, 0] - lam * p[:, :, 1]
    return jnp.einsum('bhqk,bkhe->bqhe', a.astype(v.dtype), v)


def diff_attn_mixer(xl, xc, need_ctx, *, rows, cols, w_in, lam_q1, lam_k1, lam_q2, lam_k2, subln_g, w_out, lambda_init):
    f32 = jnp.float32
    lam = (jnp.exp(jnp.sum(lam_q1.astype(f32) * lam_k1.astype(f32)))
           - jnp.exp(jnp.sum(lam_q2.astype(f32) * lam_k2.astype(f32))) + lambda_init)

    def project(x, use_rope):
        b, s, _ = x.shape
        q, k, v, g = jnp.split(x @ w_in, 4, axis=-1)
        q = q.reshape(b, s, 2 * DA_HEADS, DA_HEAD_DIM)
        k = k.reshape(b, s, 2 * DA_HEADS, DA_HEAD_DIM)
        if use_rope:
            q = rope_2d(q, rows, cols)
            k = rope_2d(k, rows, cols)
        q = q.reshape(b, s, DA_HEADS, 2, DA_HEAD_DIM)
        k = k.reshape(b, s, DA_HEADS, 2, DA_HEAD_DIM)
        v = v.reshape(b, s, DA_HEADS, DA_V_DIM)
        return q, k, v, g

    def finish(o, g):
        o = rmsnorm(o, subln_g) * (1.0 - lambda_init)
        o = o.reshape(o.shape[:2] + (D_MODEL,)) * jax.nn.silu(g)
        return o @ w_out

    ql, kl, vl, gl = project(xl, True)
    qc, kc, vc, gc = project(xc, False)
    k_all = jnp.concatenate([kl, kc], axis=1)
    v_all = jnp.concatenate([vl, vc], axis=1)
    ol = sweep_query_blocks(lambda q: diff_attn_core(q, k_all, v_all, lam), (ql,))
    yl = finish(ol, gl)
    yc = finish(diff_attn_core(qc, kc, vc, lam), gc) if need_ctx else None
    return yl, yc


def gqa_core(q, k, v):
    s = jnp.einsum('bqgrd,bkgd->bgrqk', q, k).astype(jnp.float32) * (GQ_HEAD_DIM ** -0.5)
    p = jax.nn.softmax(s, axis=-1)
    return jnp.einsum('bgrqk,bkgd->bqgrd', p.astype(v.dtype), v)


def gqa_mixer(xl, xc, need_ctx, *, rows, cols, w_in, q_norm_g, k_norm_g, w_out):
    rep = GQ_HEADS // GQ_KV_HEADS

    def project(x, use_rope):
        b, s, _ = x.shape
        q, k, v, g = jnp.split(x @ w_in, [GQ_Q_W, GQ_Q_W + GQ_KV_W, GQ_Q_W + 2 * GQ_KV_W], axis=-1)
        q = rmsnorm(q.reshape(b, s, GQ_HEADS, GQ_HEAD_DIM), q_norm_g)
        k = rmsnorm(k.reshape(b, s, GQ_KV_HEADS, GQ_HEAD_DIM), k_norm_g)
        v = v.reshape(b, s, GQ_KV_HEADS, GQ_HEAD_DIM)
        if use_rope:
            q = rope_2d(q, rows, cols)
            k = rope_2d(k, rows, cols)
        q = q.reshape(b, s, GQ_KV_HEADS, rep, GQ_HEAD_DIM)
        return q, k, v, g

    def finish(o, g):
        o = o.reshape(o.shape[:2] + (D_MODEL,)) * jax.nn.silu(g)
        return o @ w_out

    ql, kl, vl, gl = project(xl, True)
    qc, kc, vc, gc = project(xc, False)
    k_all = jnp.concatenate([kl, kc], axis=1)
    v_all = jnp.concatenate([vl, vc], axis=1)
    ol = sweep_query_blocks(lambda q: gqa_core(q, k_all, v_all), (ql,))
    yl = finish(ol, gl)
    yc = finish(gqa_core(qc, kc, vc), gc) if need_ctx else None
    return yl, yc


def mla_core(q_n, q_r, k_n, k_r, v):
    s = (jnp.einsum('bqhd,bkhd->bhqk', q_n, k_n).astype(jnp.float32)
         + jnp.einsum('bqhd,bkd->bhqk', q_r, k_r).astype(jnp.float32))
    p = jax.nn.softmax(s * ((ML_NOPE_DIM + ML_ROPE_DIM) ** -0.5), axis=-1)
    return jnp.einsum('bhqk,bkhd->bqhd', p.astype(v.dtype), v)


def mla_mixer(xl, xc, need_ctx, *, rows, cols, w_in, q_a_norm_g, w_q_b, kv_a_norm_g, w_kv_b, w_out):
    def project(x, use_rope):
        b, s, _ = x.shape
        q_a, kv_a, k_r, g = jnp.split(
            x @ w_in, [ML_Q_RANK, ML_Q_RANK + ML_KV_RANK, ML_Q_RANK + ML_KV_RANK + ML_ROPE_DIM], axis=-1)
        q = (rmsnorm(q_a, q_a_norm_g) @ w_q_b).reshape(b, s, ML_HEADS, ML_NOPE_DIM + ML_ROPE_DIM)
        q_n, q_r = q[..., :ML_NOPE_DIM], q[..., ML_NOPE_DIM:]
        kv = (rmsnorm(kv_a, kv_a_norm_g) @ w_kv_b).reshape(b, s, ML_HEADS, ML_NOPE_DIM + ML_V_DIM)
        k_n, v = kv[..., :ML_NOPE_DIM], kv[..., ML_NOPE_DIM:]
        k_r = k_r[:, :, None, :]
        if use_rope:
            q_r = rope_2d(q_r, rows, cols)
            k_r = rope_2d(k_r, rows, cols)
        return q_n, q_r, k_n, k_r[:, :, 0, :], v, g

    def finish(o, g):
        o = o.reshape(o.shape[:2] + (ML_HEADS * ML_V_DIM,)) * jax.nn.silu(g)
        return o @ w_out

    qnl, qrl, knl, krl, vl, gl = project(xl, True)
    qnc, qrc, knc, krc, vc, gc = project(xc, False)
    kn_all = jnp.concatenate([knl, knc], axis=1)
    kr_all = jnp.concatenate([krl, krc], axis=1)
    v_all = jnp.concatenate([vl, vc], axis=1)
    ol = sweep_query_blocks(lambda qn, qr: mla_core(qn, qr, kn_all, kr_all, v_all), (qnl, qrl))
    yl = finish(ol, gl)
    yc = finish(mla_core(qnc, qrc, knc, krc, vc), gc) if need_ctx else None
    return yl, yc


def sandwich_layer(h_lat, h_ctx, c, c_ctx, ada_w, ada_b, pre_g, post_g, mixer_fn, need_ctx):
    shift, scale, gate = jnp.split(jax.nn.silu(c)[:, None, :] @ ada_w + ada_b, 3, axis=-1)
    shift_c, scale_c, gate_c = jnp.split(jax.nn.silu(c_ctx)[None, None, :] @ ada_w + ada_b, 3, axis=-1)
    xl = rmsnorm(h_lat, pre_g) * (1.0 + scale) + shift
    xc = rmsnorm(h_ctx, pre_g) * (1.0 + scale_c) + shift_c
    yl, yc = mixer_fn(xl, xc, need_ctx)
    h_lat = h_lat + gate * rmsnorm(yl, post_g)
    if need_ctx:
        h_ctx = h_ctx + gate_c * rmsnorm(yc, post_g)
    return h_lat, h_ctx


def setup_inputs(seed: int = 0) -> dict:
    key = jax.random.key(seed)
    keys = iter(jax.random.split(key, 64))
    f32 = jnp.float32

    def dense(fan_in, fan_out, gain=1.0):
        return jax.random.normal(next(keys), (fan_in, fan_out), f32) * (gain * fan_in ** -0.5)

    def norm_gain(n):
        return 1.0 + 0.02 * jax.random.normal(next(keys), (n,), f32)

    def small(n, s):
        return s * jax.random.normal(next(keys), (n,), f32)

    p = {
        'x': jax.random.normal(next(keys), (BATCH, SEQ, D_MODEL), f32),
        'c': jax.random.normal(next(keys), (BATCH, D_MODEL), f32),
        'ctx': jax.random.normal(next(keys), (BATCH, CTX_LEN, D_MODEL), f32),
        'c_ctx': jax.random.normal(next(keys), (D_MODEL,), f32),
    }
    for l in range(DEPTH):
        kind = l % N_MIXERS
        pre = 'l%d_' % l
        p[pre + 'ada_w'] = dense(D_MODEL, 3 * D_MODEL, 0.5)
        p[pre + 'ada_b'] = small(3 * D_MODEL, 0.01)
        p[pre + 'pre_g'] = norm_gain(D_MODEL)
        p[pre + 'post_g'] = norm_gain(D_MODEL)
        if kind == 0:
            p[pre + 'w_in'] = dense(D_MODEL, 4 * D_MODEL)
            for name in ('lam_q1', 'lam_k1', 'lam_q2', 'lam_k2'):
                p[pre + name] = small(DA_HEAD_DIM, 0.1)
            p[pre + 'subln_g'] = norm_gain(DA_V_DIM)
        elif kind == 1:
            p[pre + 'w_in'] = dense(D_MODEL, GQ_Q_W + 2 * GQ_KV_W + D_MODEL)
            p[pre + 'q_norm_g'] = norm_gain(GQ_HEAD_DIM)
            p[pre + 'k_norm_g'] = norm_gain(GQ_HEAD_DIM)
        else:
            p[pre + 'w_in'] = dense(D_MODEL, ML_Q_RANK + ML_KV_RANK + ML_ROPE_DIM + D_MODEL)
            p[pre + 'q_a_norm_g'] = norm_gain(ML_Q_RANK)
            p[pre + 'w_q_b'] = dense(ML_Q_RANK, ML_HEADS * (ML_NOPE_DIM + ML_ROPE_DIM))
            p[pre + 'kv_a_norm_g'] = norm_gain(ML_KV_RANK)
            p[pre + 'w_kv_b'] = dense(ML_KV_RANK, ML_HEADS * (ML_NOPE_DIM + ML_V_DIM))
        p[pre + 'w_out'] = dense(D_MODEL, D_MODEL)
    return p


def reference(x, c, ctx, c_ctx,
              l0_ada_w, l0_ada_b, l0_pre_g, l0_post_g, l0_w_in, l0_lam_q1, l0_lam_k1, l0_lam_q2, l0_lam_k2, l0_subln_g, l0_w_out,
              l1_ada_w, l1_ada_b, l1_pre_g, l1_post_g, l1_w_in, l1_q_norm_g, l1_k_norm_g, l1_w_out,
              l2_ada_w, l2_ada_b, l2_pre_g, l2_post_g, l2_w_in, l2_q_a_norm_g, l2_w_q_b, l2_kv_a_norm_g, l2_w_kv_b, l2_w_out,
              l3_ada_w, l3_ada_b, l3_pre_g, l3_post_g, l3_w_in, l3_lam_q1, l3_lam_k1, l3_lam_q2, l3_lam_k2, l3_subln_g, l3_w_out):
    rows, cols = grid_positions(x.shape[1])
    common = [
        (l0_ada_w, l0_ada_b, l0_pre_g, l0_post_g),
        (l1_ada_w, l1_ada_b, l1_pre_g, l1_post_g),
        (l2_ada_w, l2_ada_b, l2_pre_g, l2_post_g),
        (l3_ada_w, l3_ada_b, l3_pre_g, l3_post_g),
    ]
    mixers = [
        functools.partial(diff_attn_mixer, rows=rows, cols=cols, w_in=l0_w_in, lam_q1=l0_lam_q1, lam_k1=l0_lam_k1,
                          lam_q2=l0_lam_q2, lam_k2=l0_lam_k2, subln_g=l0_subln_g, w_out=l0_w_out,
                          lambda_init=lambda_init_fn(0)),
        functools.partial(gqa_mixer, rows=rows, cols=cols, w_in=l1_w_in, q_norm_g=l1_q_norm_g,
                          k_norm_g=l1_k_norm_g, w_out=l1_w_out),
        functools.partial(mla_mixer, rows=rows, cols=cols, w_in=l2_w_in, q_a_norm_g=l2_q_a_norm_g, w_q_b=l2_w_q_b,
                          kv_a_norm_g=l2_kv_a_norm_g, w_kv_b=l2_w_kv_b, w_out=l2_w_out),
        functools.partial(diff_attn_mixer, rows=rows, cols=cols, w_in=l3_w_in, lam_q1=l3_lam_q1, lam_k1=l3_lam_k1,
                          lam_q2=l3_lam_q2, lam_k2=l3_lam_k2, subln_g=l3_subln_g, w_out=l3_w_out,
                          lambda_init=lambda_init_fn(3)),
    ]
    h, hc = x, ctx
    for l in range(DEPTH):
        h, hc = sandwich_layer(h, hc, c, c_ctx, *common[l], mixers[l], l < DEPTH - 1)
    return h
```

```python
import functools
import math

import jax
import jax.numpy as jnp
from jax import lax
from jax.experimental import pallas as pl
from jax.experimental.pallas import tpu as pltpu

F32 = jnp.float32
BF16 = jnp.bfloat16

D_MODEL = 2048
GRID_W = 64
ROPE_BASE = 10000.0
NORM_EPS = 1e-6
DA_HEAD_DIM = 64
DA_HEADS = D_MODEL // (2 * DA_HEAD_DIM)
GQ_HEAD_DIM = 128
GQ_HEADS = D_MODEL // GQ_HEAD_DIM
GQ_KV_HEADS = GQ_HEADS // 4
GQ_KV_W = GQ_KV_HEADS * GQ_HEAD_DIM
ML_NOPE_DIM = 128
ML_ROPE_DIM = 64
ML_V_DIM = 128
ML_HEADS = D_MODEL // ML_V_DIM
ML_Q_RANK = D_MODEL // 4
ML_KV_RANK = D_MODEL // 4

LANES = 128
ROW_CHUNK = 128
VMEM_LIMIT = 56 * 1024 * 1024
MOD_ROWS = 16

PROJ_TM = 768
FINISH_TM = 384
FINISH_TM_LAST = 512
ATTN_TQ = 256


def _silu(x):
    return x / (1.0 + jnp.exp(-x))


def _rms(x, eps=NORM_EPS):
    return x * lax.rsqrt(jnp.mean(x * x, axis=-1, keepdims=True) + eps)


def _rope(x, cos, sin_a, sin_b, half):
    return (x * cos + pltpu.roll(x, LANES - half, 1) * sin_a + pltpu.roll(x, half, 1) * sin_b)


def _nt_dot(a, b):
    return lax.dot_general(a, b, (((1,), (1,)), ((), ())), preferred_element_type=F32)


def _rope_tables(seq, ctx_len, head_dim):
    t = seq + ctx_len
    r = jnp.arange(t)
    lat = (r < seq)[:, None]
    rowpos = (r // GRID_W).astype(F32)[:, None]
    colpos = (r % GRID_W).astype(F32)[:, None]
    lane = jnp.arange(LANES)
    u = lane % head_dim
    half2 = head_dim // 2
    half = half2 // 2
    grp = (u // half2)[None, :]
    w = u % half2
    first = (w < half)[None, :]
    inv_freq = (ROPE_BASE ** (-(w % half).astype(F32) / half))[None, :]
    ang = jnp.where(grp == 0, rowpos, colpos) * inv_freq
    cos = jnp.where(lat, jnp.cos(ang), 1.0)
    sin = jnp.where(lat, jnp.sin(ang), 0.0)
    sin_a = jnp.where(first, -sin, 0.0)
    sin_b = jnp.where(first, 0.0, sin)
    return cos.astype(F32), sin_a.astype(F32), sin_b.astype(F32)


def _ada_kernel(c_ref, w_ref, b_ref, o_ref):
    s = _silu(c_ref[...])
    o_ref[...] = jnp.dot(s.astype(BF16), w_ref[...].astype(BF16),
                         preferred_element_type=F32) + b_ref[...]


def _ada_call(cc, ada_w, ada_b):
    d, n = ada_w.shape
    tn = 768
    return pl.pallas_call(
        _ada_kernel,
        grid=(n // tn,),
        in_specs=[pl.BlockSpec((MOD_ROWS, d), lambda j: (0, 0)),
                  pl.BlockSpec((d, tn), lambda j: (0, j)),
                  pl.BlockSpec((1, tn), lambda j: (0, j))],
        out_specs=pl.BlockSpec((MOD_ROWS, tn), lambda j: (0, j)),
        out_shape=jax.ShapeDtypeStruct((MOD_ROWS, n), F32),
        compiler_params=pltpu.CompilerParams(dimension_semantics=("arbitrary",),
                                             vmem_limit_bytes=VMEM_LIMIT),
        name="ada",
    )(cc, ada_w, ada_b.reshape(1, n))


def _mod_rows(mod_b_ref, mod_c_ref, is_ctx, part):
    d = D_MODEL
    mb = mod_b_ref[:, part * d:(part + 1) * d]
    mc = mod_c_ref[:, part * d:(part + 1) * d]
    return jnp.where(is_ctx, mc, mb)


def _prologue(h_ref, mod_b_ref, mod_c_ref, preg_ref, xn_ref, row0, seq):
    tm = h_ref.shape[0]
    preg = preg_ref[...]
    for c in range(tm // ROW_CHUNK):
        rows = pl.ds(c * ROW_CHUNK, ROW_CHUNK)
        is_ctx = (row0 + c * ROW_CHUNK) >= seq
        shift = _mod_rows(mod_b_ref, mod_c_ref, is_ctx, 0)
        scale = _mod_rows(mod_b_ref, mod_c_ref, is_ctx, 1)
        y = _rms(h_ref[rows, :]) * preg
        xn_ref[rows, :] = (y * (1.0 + scale) + shift).astype(BF16)


def _proj_kernel(*refs, kinds, n_qkv_tiles, seq, tiles_per_batch, q_scale, rope_half):
    (h_ref, mod_b_ref, mod_c_ref, preg_ref, w_ref, cos_ref, sa_ref, sb_ref,
     ng0_ref, ng1_ref, o_ref, g_ref, xn_ref) = refs
    i = pl.program_id(0)
    j = pl.program_id(1)
    tm = h_ref.shape[0]
    tn = w_ref.shape[1]

    @pl.when(j == 0)
    def _():
        _prologue(h_ref, mod_b_ref, mod_c_ref, preg_ref, xn_ref,
                  (i % tiles_per_batch) * tm, seq)

    def acc():
        return jnp.dot(xn_ref[...], w_ref[...], preferred_element_type=F32)

    def rope(x):
        return _rope(x, cos_ref[...], sa_ref[...], sb_ref[...], rope_half)

    def chunks(a):
        return [a[:, c * LANES:(c + 1) * LANES] for c in range(tn // LANES)]

    def store(parts):
        for c, p in enumerate(parts):
            o_ref[:, c * LANES:(c + 1) * LANES] = p.astype(BF16)

    def tile(kind):
        a = acc()
        if kind == "g":
            g_ref[...] = a
        elif kind == "v":
            o_ref[...] = a.astype(BF16)
        elif kind == "da_q":
            store([rope(x) * q_scale for x in chunks(a)])
        elif kind == "da_k":
            store([rope(x) for x in chunks(a)])
        elif kind == "gq_q":
            store([rope(_rms(x) * ng0_ref[...]) * q_scale for x in chunks(a)])
        elif kind == "gq_kv":
            cs = chunks(a)
            nk = GQ_KV_W // LANES
            store([rope(_rms(x) * ng1_ref[...]) for x in cs[:nk]] + cs[nk:])
        elif kind == "ml_qa":
            o_ref[...] = (_rms(a) * ng0_ref[...]).astype(BF16)
        elif kind == "ml_kva":
            o_ref[...] = (_rms(a) * ng1_ref[...]).astype(BF16)
        elif kind == "ml_kr":
            cs = chunks(a)
            store([rope(cs[0])] + cs[1:])
        else:
            raise ValueError(kind)

    for kind in sorted(set(kinds)):
        js = [t for t, k in enumerate(kinds) if k == kind]
        lo, hi = js[0], js[-1]
        assert js == list(range(lo, hi + 1))
        pl.when((j >= lo) & (j <= hi))(functools.partial(tile, kind))


def _proj_call(h2, mod3, pre_g, w, tables, ng0, ng1, *, kinds, tn, batch, seq, ctx_len,
               q_scale, rope_half):
    m, d = h2.shape
    n = w.shape[1]
    t = seq + ctx_len
    tm = PROJ_TM
    tpb = t // tm
    assert t % tm == 0 and n % tn == 0 and len(kinds) == n // tn
    n_g = sum(k == "g" for k in kinds)
    n_qkv = len(kinds) - n_g
    assert all(k == "g" for k in kinds[n_qkv:]) and n_g * tn == d
    cos, sa, sb = tables
    kern = functools.partial(_proj_kernel, kinds=tuple(kinds), n_qkv_tiles=n_qkv, seq=seq,
                             tiles_per_batch=tpb, q_scale=q_scale, rope_half=rope_half)
    tab_spec = pl.BlockSpec((tm, LANES), lambda i, j: (i % tpb, 0))
    return pl.pallas_call(
        kern,
        grid=(m // tm, n // tn),
        in_specs=[
            pl.BlockSpec((tm, d), lambda i, j: (i, 0)),
            pl.BlockSpec((None, 1, 3 * d), lambda i, j: (i // tpb, 0, 0)),
            pl.BlockSpec((None, 1, 3 * d), lambda i, j: (batch, 0, 0)),
            pl.BlockSpec((1, d), lambda i, j: (0, 0)),
            pl.BlockSpec((d, tn), lambda i, j: (0, j)),
            tab_spec, tab_spec, tab_spec,
            pl.BlockSpec(ng0.shape, lambda i, j: (0, 0)),
            pl.BlockSpec(ng1.shape, lambda i, j: (0, 0)),
        ],
        out_specs=[
            pl.BlockSpec((tm, tn), lambda i, j: (i, jnp.minimum(j, n_qkv - 1))),
            pl.BlockSpec((tm, tn), lambda i, j: (i, jnp.maximum(j - n_qkv, 0))),
        ],
        out_shape=[jax.ShapeDtypeStruct((m, n_qkv * tn), BF16),
                   jax.ShapeDtypeStruct((m, d), F32)],
        scratch_shapes=[pltpu.VMEM((tm, d), BF16)],
        compiler_params=pltpu.CompilerParams(dimension_semantics=("arbitrary", "arbitrary"),
                                             vmem_limit_bytes=VMEM_LIMIT),
        name="proj",
    )(h2, mod3, mod3, pre_g.reshape(1, d), w, cos, sa, sb, ng0, ng1)


def _mla_b_kernel(qa_ref, kva_ref, wq_ref, wkn_ref, wv_ref, cos_ref, sa_ref, sb_ref,
                  q_ref, kn_ref, v_ref, *, q_scale):
    q = jnp.dot(qa_ref[...], wq_ref[...], preferred_element_type=F32)
    for c in range(q.shape[1] // LANES):
        x = q[:, c * LANES:(c + 1) * LANES]
        if c % 2 == 1:
            x = _rope(x, cos_ref[...], sa_ref[...], sb_ref[...], ML_ROPE_DIM // 4)
        q_ref[:, c * LANES:(c + 1) * LANES] = (x * q_scale).astype(BF16)
    kva = kva_ref[...]
    kn_ref[...] = jnp.dot(kva, wkn_ref[...], preferred_element_type=F32).astype(BF16)
    v_ref[...] = jnp.dot(kva, wv_ref[...], preferred_element_type=F32).astype(BF16)


def _mla_b_call(qkv, wq, wkv, tables, *, seq, ctx_len, q_scale):
    m = qkv.shape[0]
    tm = PROJ_TM
    tpb = (seq + ctx_len) // tm
    r = ML_Q_RANK
    nq = wq.shape[1]
    nkv = wkv.shape[1] // 2
    steps = 4
    tq, tk = nq // steps, nkv // steps
    cos, sa, sb = tables
    tab_spec = pl.BlockSpec((tm, LANES), lambda i, j: (i % tpb, 0))
    return pl.pallas_call(
        functools.partial(_mla_b_kernel, q_scale=q_scale),
        grid=(m // tm, steps),
        in_specs=[
            pl.BlockSpec((tm, r), lambda i, j: (i, 0)),
            pl.BlockSpec((tm, r), lambda i, j: (i, 1)),
            pl.BlockSpec((r, tq), lambda i, j: (0, j)),
            pl.BlockSpec((r, tk), lambda i, j: (0, j)),
            pl.BlockSpec((r, tk), lambda i, j: (0, steps + j)),
            tab_spec, tab_spec, tab_spec,
        ],
        out_specs=[
            pl.BlockSpec((tm, tq), lambda i, j: (i, j)),
            pl.BlockSpec((tm, tk), lambda i, j: (i, j)),
            pl.BlockSpec((tm, tk), lambda i, j: (i, j)),
        ],
        out_shape=[jax.ShapeDtypeStruct((m, nq), BF16),
                   jax.ShapeDtypeStruct((m, nkv), BF16),
                   jax.ShapeDtypeStruct((m, nkv), BF16)],
        compiler_params=pltpu.CompilerParams(dimension_semantics=("arbitrary", "arbitrary"),
                                             vmem_limit_bytes=VMEM_LIMIT),
        name="mla_b",
    )(qkv, qkv, wq, wkv, wkv, cos, sa, sb)


def _softmax_pv(s, v):
    m = jnp.max(s, axis=-1, keepdims=True)
    e = jnp.exp(s - m)
    l = jnp.sum(e, axis=-1, keepdims=True)
    return jnp.dot(e.astype(BF16), v, preferred_element_type=F32), l


def _diff_attn_kernel(lq1_ref, lk1_ref, lq2_ref, lk2_ref, subg_ref, q_ref, k_ref, v_ref, g_ref,
                      o_ref, *, seq, n_lat_tiles, lambda_init):
    qi = pl.program_id(2)
    tq = q_ref.shape[0]
    lam = (jnp.exp(jnp.sum(lq1_ref[...] * lk1_ref[...], axis=-1, keepdims=True))
           - jnp.exp(jnp.sum(lq2_ref[...] * lk2_ref[...], axis=-1, keepdims=True))
           + lambda_init)

    def body(k, v):
        q = q_ref[...].astype(F32)
        lane = lax.broadcasted_iota(jnp.int32, q.shape, 1)
        q0 = jnp.where(lane < DA_HEAD_DIM, q, 0.0).astype(BF16)
        q1 = jnp.where(lane >= DA_HEAD_DIM, q, 0.0).astype(BF16)
        s = _nt_dot(jnp.concatenate([q0, q1], axis=0), k)
        o2, l = _softmax_pv(s, v)
        o2 = o2 / l
        o = o2[:tq] - lam * o2[tq:]
        o = _rms(o) * subg_ref[...] * (1.0 - lambda_init)
        o_ref[...] = (o * _silu(g_ref[...])).astype(BF16)

    @pl.when(qi < n_lat_tiles)
    def _():
        body(k_ref[...], v_ref[...])

    @pl.when(qi >= n_lat_tiles)
    def _():
        body(k_ref[seq:, :], v_ref[seq:, :])


def _gqa_attn_kernel(q_ref, k_ref, v_ref, g_ref, o_ref, *, seq, n_lat_tiles):
    qi = pl.program_id(2)

    def body(k, v):
        o, l = _softmax_pv(_nt_dot(q_ref[...], k), v)
        o_ref[...] = ((o / l) * _silu(g_ref[...])).astype(BF16)

    @pl.when(qi < n_lat_tiles)
    def _():
        body(k_ref[...], v_ref[...])

    @pl.when(qi >= n_lat_tiles)
    def _():
        body(k_ref[seq:, :], v_ref[seq:, :])


def _mla_attn_kernel(q_ref, kn_ref, kr_ref, v_ref, g_ref, o_ref, kcat_ref, *, seq, n_lat_tiles):
    qi = pl.program_id(2)

    @pl.when(qi == 0)
    def _():
        kcat_ref[:, :LANES] = kn_ref[...]
        kcat_ref[:, LANES:] = kr_ref[...]

    def body(k, v):
        o, l = _softmax_pv(_nt_dot(q_ref[...], k), v)
        o_ref[...] = ((o / l) * _silu(g_ref[...])).astype(BF16)

    @pl.when(qi < n_lat_tiles)
    def _():
        body(kcat_ref[...], v_ref[...])

    @pl.when(qi >= n_lat_tiles)
    def _():
        body(kcat_ref[seq:, :], v_ref[seq:, :])


def _attn_call(kind, arrays, small, *, batch, seq, ctx_len, need_ctx, lambda_init=None):
    t = seq + ctx_len
    tq = ATTN_TQ
    n_lat = seq // tq
    nq = t // tq
    n_q_tiles = nq if need_ctx else n_lat
    m = batch * t
    heads = D_MODEL // LANES

    def row_spec(width, col):
        return pl.BlockSpec((tq, width), lambda b, h, qi: (b * nq + qi, col(h)))

    def kv_spec(col):
        return pl.BlockSpec((t, LANES), lambda b, h, qi: (b, col(h)))

    def small_spec(a):
        return pl.BlockSpec(a.shape, lambda b, h, qi: (0,) * a.ndim)

    g_spec = row_spec(LANES, lambda h: h)
    out_spec = row_spec(LANES, lambda h: h)
    scratch = []
    if kind == "diff":
        qkv, g = arrays["qkv"], arrays["g"]
        kern = functools.partial(_diff_attn_kernel, seq=seq, n_lat_tiles=n_lat,
                                 lambda_init=lambda_init)
        in_specs = [small_spec(a) for a in small] + [
            row_spec(LANES, lambda h: h),
            kv_spec(lambda h: heads + h),
            kv_spec(lambda h: 2 * heads + h),
            g_spec]
        args = list(small) + [qkv, qkv, qkv, g]
    elif kind == "gqa":
        qkv, g = arrays["qkv"], arrays["g"]
        rep = GQ_HEADS // GQ_KV_HEADS
        kern = functools.partial(_gqa_attn_kernel, seq=seq, n_lat_tiles=n_lat)
        in_specs = [
            row_spec(LANES, lambda h: h),
            kv_spec(lambda h: GQ_HEADS + h // rep),
            kv_spec(lambda h: GQ_HEADS + GQ_KV_HEADS + h // rep),
            g_spec]
        args = [qkv, qkv, qkv, g]
    elif kind == "mla":
        kern = functools.partial(_mla_attn_kernel, seq=seq, n_lat_tiles=n_lat)
        kr_col = (ML_Q_RANK + ML_KV_RANK) // LANES
        in_specs = [
            row_spec(2 * LANES, lambda h: h),
            kv_spec(lambda h: h),
            kv_spec(lambda h: kr_col),
            kv_spec(lambda h: h),
            g_spec]
        args = [arrays["q"], arrays["kn"], arrays["qkv"], arrays["v"], arrays["g"]]
        scratch = [pltpu.VMEM((t, 2 * LANES), BF16)]
    else:
        raise ValueError(kind)
    return pl.pallas_call(
        kern,
        grid=(batch, heads, n_q_tiles),
        in_specs=in_specs,
        out_specs=out_spec,
        out_shape=jax.ShapeDtypeStruct((m, D_MODEL), BF16),
        scratch_shapes=scratch,
        compiler_params=pltpu.CompilerParams(
            dimension_semantics=("arbitrary", "arbitrary", "arbitrary"),
            vmem_limit_bytes=VMEM_LIMIT),
        name="attn_" + kind,
    )(*args)


def _finish_kernel(a_ref, h_ref, mod_b_ref, mod_c_ref, postg_ref, w_ref, o_ref, *, seq):
    ti = pl.program_id(1)
    tm = a_ref.shape[0]
    y = jnp.dot(a_ref[...], w_ref[...], preferred_element_type=F32)
    postg = postg_ref[...]
    for c in range(tm // ROW_CHUNK):
        rows = slice(c * ROW_CHUNK, (c + 1) * ROW_CHUNK)
        is_ctx = (ti * tm + c * ROW_CHUNK) >= seq
        gate = _mod_rows(mod_b_ref, mod_c_ref, is_ctx, 2)
        o_ref[rows, :] = h_ref[rows, :] + gate * (_rms(y[rows, :]) * postg)


def _finish_call(a3, h3, mod3, post_g, w, *, seq, need_ctx):
    batch, t, d = h3.shape
    tm = FINISH_TM if need_ctx else FINISH_TM_LAST
    rows_out = t if need_ctx else seq
    assert rows_out % tm == 0
    blk = pl.BlockSpec((None, tm, d), lambda b, i: (b, i, 0))
    return pl.pallas_call(
        functools.partial(_finish_kernel, seq=seq),
        grid=(batch, rows_out // tm),
        in_specs=[
            blk, blk,
            pl.BlockSpec((None, 1, 3 * d), lambda b, i: (b, 0, 0)),
            pl.BlockSpec((None, 1, 3 * d), lambda b, i: (batch, 0, 0)),
            pl.BlockSpec((1, d), lambda b, i: (0, 0)),
            pl.BlockSpec((d, d), lambda b, i: (0, 0)),
        ],
        out_specs=blk,
        out_shape=jax.ShapeDtypeStruct((batch, rows_out, d), F32),
        compiler_params=pltpu.CompilerParams(dimension_semantics=("arbitrary", "arbitrary"),
                                             vmem_limit_bytes=VMEM_LIMIT),
        name="finish",
    )(a3, h3, mod3, mod3, post_g.reshape(1, d), w)


def _lambda_init(layer):
    return 0.8 - 0.6 * math.exp(-0.3 * layer)


def _layer(kind, layer, h3, cc, p, need_ctx, tables, *, seq, ctx_len):
    batch, t, d = h3.shape
    m = batch * t
    mod3 = _ada_call(cc, p["ada_w"], p["ada_b"]).reshape(MOD_ROWS, 1, 3 * d)
    h2 = h3.reshape(m, d)
    one = jnp.ones((1, LANES), F32)
    common = dict(batch=batch, seq=seq, ctx_len=ctx_len)
    if kind == "diff":
        qkv, g = _proj_call(h2, mod3, p["pre_g"], p["w_in"].astype(BF16), tables[DA_HEAD_DIM],
                            one, one, kinds=["da_q"] * 2 + ["da_k"] * 2 + ["v"] * 2 + ["g"] * 2,
                            tn=1024, q_scale=DA_HEAD_DIM ** -0.5, rope_half=DA_HEAD_DIM // 4,
                            **common)
        small = [p[n].reshape(1, DA_HEAD_DIM) for n in ("lam_q1", "lam_k1", "lam_q2", "lam_k2")]
        small.append(p["subln_g"].reshape(1, 2 * DA_HEAD_DIM))
        a = _attn_call("diff", dict(qkv=qkv, g=g), small, need_ctx=need_ctx,
                       lambda_init=_lambda_init(layer), **common)
    elif kind == "gqa":
        qkv, g = _proj_call(h2, mod3, p["pre_g"], p["w_in"].astype(BF16), tables[GQ_HEAD_DIM],
                            p["q_norm_g"].reshape(1, GQ_HEAD_DIM),
                            p["k_norm_g"].reshape(1, GQ_HEAD_DIM),
                            kinds=["gq_q"] * 2 + ["gq_kv"] + ["g"] * 2,
                            tn=1024, q_scale=GQ_HEAD_DIM ** -0.5, rope_half=GQ_HEAD_DIM // 4,
                            **common)
        a = _attn_call("gqa", dict(qkv=qkv, g=g), [], need_ctx=need_ctx, **common)
    elif kind == "mla":
        w_in = p["w_in"]
        r2 = ML_Q_RANK + ML_KV_RANK
        tn = ML_Q_RANK
        w_pad = jnp.concatenate(
            [w_in[:, :r2 + ML_ROPE_DIM], jnp.zeros((d, tn - ML_ROPE_DIM), F32),
             w_in[:, r2 + ML_ROPE_DIM:]], axis=1).astype(BF16)
        qkv, g = _proj_call(h2, mod3, p["pre_g"], w_pad, tables[ML_ROPE_DIM],
                            p["q_a_norm_g"].reshape(1, ML_Q_RANK),
                            p["kv_a_norm_g"].reshape(1, ML_KV_RANK),
                            kinds=["ml_qa", "ml_kva", "ml_kr"] + ["g"] * (d // tn),
                            tn=tn, q_scale=1.0, rope_half=ML_ROPE_DIM // 4, **common)
        qd = ML_NOPE_DIM + ML_ROPE_DIM
        wq = p["w_q_b"].reshape(ML_Q_RANK, ML_HEADS, qd)
        wq = jnp.pad(wq, ((0, 0), (0, 0), (0, 2 * LANES - qd)))
        wq = wq.reshape(ML_Q_RANK, ML_HEADS * 2 * LANES).astype(BF16)
        wkv = p["w_kv_b"].reshape(ML_KV_RANK, ML_HEADS, ML_NOPE_DIM + ML_V_DIM)
        wkv = jnp.concatenate([wkv[:, :, :ML_NOPE_DIM].reshape(ML_KV_RANK, -1),
                               wkv[:, :, ML_NOPE_DIM:].reshape(ML_KV_RANK, -1)],
                              axis=1).astype(BF16)
        q, kn, v = _mla_b_call(qkv, wq, wkv, tables[ML_ROPE_DIM], seq=seq, ctx_len=ctx_len,
                               q_scale=qd ** -0.5)
        a = _attn_call("mla", dict(q=q, kn=kn, v=v, qkv=qkv, g=g), [], need_ctx=need_ctx,
                       **common)
    else:
        raise ValueError(kind)
    return _finish_call(a.reshape(batch, t, d), h3, mod3, p["post_g"], p["w_out"].astype(BF16),
                        seq=seq, need_ctx=need_ctx)


def kernel(x, c, ctx, c_ctx, l0_ada_w, l0_ada_b, l0_pre_g, l0_post_g, l0_w_in, l0_lam_q1, l0_lam_k1, l0_lam_q2, l0_lam_k2, l0_subln_g, l0_w_out, l1_ada_w, l1_ada_b, l1_pre_g, l1_post_g, l1_w_in, l1_q_norm_g, l1_k_norm_g, l1_w_out, l2_ada_w, l2_ada_b, l2_pre_g, l2_post_g, l2_w_in, l2_q_a_norm_g, l2_w_q_b, l2_kv_a_norm_g, l2_w_kv_b, l2_w_out, l3_ada_w, l3_ada_b, l3_pre_g, l3_post_g, l3_w_in, l3_lam_q1, l3_lam_k1, l3_lam_q2, l3_lam_k2, l3_subln_g, l3_w_out):
    batch, seq, d = x.shape
    ctx_len = ctx.shape[1]
    assert d == D_MODEL and seq % GRID_W == 0 and batch + 1 <= MOD_ROWS
    diff_names = ("ada_w", "ada_b", "pre_g", "post_g", "w_in", "lam_q1", "lam_k1", "lam_q2",
                  "lam_k2", "subln_g", "w_out")
    layers = [
        ("diff", dict(zip(diff_names, (l0_ada_w, l0_ada_b, l0_pre_g, l0_post_g, l0_w_in,
                                       l0_lam_q1, l0_lam_k1, l0_lam_q2, l0_lam_k2, l0_subln_g,
                                       l0_w_out)))),
        ("gqa", dict(ada_w=l1_ada_w, ada_b=l1_ada_b, pre_g=l1_pre_g, post_g=l1_post_g,
                     w_in=l1_w_in, q_norm_g=l1_q_norm_g, k_norm_g=l1_k_norm_g, w_out=l1_w_out)),
        ("mla", dict(ada_w=l2_ada_w, ada_b=l2_ada_b, pre_g=l2_pre_g, post_g=l2_post_g,
                     w_in=l2_w_in, q_a_norm_g=l2_q_a_norm_g, w_q_b=l2_w_q_b,
                     kv_a_norm_g=l2_kv_a_norm_g, w_kv_b=l2_w_kv_b, w_out=l2_w_out)),
        ("diff", dict(zip(diff_names, (l3_ada_w, l3_ada_b, l3_pre_g, l3_post_g, l3_w_in,
                                       l3_lam_q1, l3_lam_k1, l3_lam_q2, l3_lam_k2, l3_subln_g,
                                       l3_w_out)))),
    ]
    cc = jnp.zeros((MOD_ROWS, d), F32).at[:batch].set(c).at[batch].set(c_ctx)
    tables = {hd: _rope_tables(seq, ctx_len, hd) for hd in (DA_HEAD_DIM, GQ_HEAD_DIM)}
    h3 = jnp.concatenate([x, ctx], axis=1)
    for layer, (kind, p) in enumerate(layers):
        h3 = _layer(kind, layer, h3, cc, p, layer < len(layers) - 1, tables,
                    seq=seq, ctx_len=ctx_len)
    return h3
```

```python
import functools
import math

import jax
import jax.numpy as jnp
from jax import lax
from jax.experimental import pallas as pl
from jax.experimental.pallas import tpu as pltpu

F32 = jnp.float32
BF16 = jnp.bfloat16
LOG2_E = math.log2(math.e)

D_MODEL = 2048
GRID_W = 64
ROPE_BASE = 10000.0
NORM_EPS = 1e-6
DA_HEAD_DIM = 64
DA_HEADS = D_MODEL // (2 * DA_HEAD_DIM)
GQ_HEAD_DIM = 128
GQ_HEADS = D_MODEL // GQ_HEAD_DIM
GQ_KV_HEADS = GQ_HEADS // 4
GQ_KV_W = GQ_KV_HEADS * GQ_HEAD_DIM
ML_NOPE_DIM = 128
ML_ROPE_DIM = 64
ML_V_DIM = 128
ML_HEADS = D_MODEL // ML_V_DIM
ML_Q_RANK = D_MODEL // 4
ML_KV_RANK = D_MODEL // 4

LANES = 128
MXU_N = 512
ROW_CHUNK = 128
VMEM_LIMIT = 56 * 1024 * 1024
MOD_ROWS = 16

PROJ_TM = 768
FINISH_TM = 384
FINISH_TM_LAST = 512
MLA_HEADS_PER_STEP = 2


def _silu(x):
    return x / (1.0 + jnp.exp(-x))


def _rms(x, eps=NORM_EPS):
    return x * lax.rsqrt(jnp.mean(x * x, axis=-1, keepdims=True) + eps)


def _rope(x, cos, sin_a, sin_b, half):
    return (x * cos + pltpu.roll(x, LANES - half, 1) * sin_a + pltpu.roll(x, half, 1) * sin_b)


def _nt_dot(a, b):
    return lax.dot_general(a, b, (((1,), (1,)), ((), ())), preferred_element_type=F32)


def _rope_tables(seq, ctx_len, head_dim):
    t = seq + ctx_len
    r = jnp.arange(t)
    lat = (r < seq)[:, None]
    rowpos = (r // GRID_W).astype(F32)[:, None]
    colpos = (r % GRID_W).astype(F32)[:, None]
    lane = jnp.arange(LANES)
    u = lane % head_dim
    half2 = head_dim // 2
    half = half2 // 2
    grp = (u // half2)[None, :]
    w = u % half2
    first = (w < half)[None, :]
    inv_freq = (ROPE_BASE ** (-(w % half).astype(F32) / half))[None, :]
    ang = jnp.where(grp == 0, rowpos, colpos) * inv_freq
    cos = jnp.where(lat, jnp.cos(ang), 1.0)
    sin = jnp.where(lat, jnp.sin(ang), 0.0)
    sin_a = jnp.where(first, -sin, 0.0)
    sin_b = jnp.where(first, 0.0, sin)
    return cos.astype(F32), sin_a.astype(F32), sin_b.astype(F32)


def _ada_kernel(c_ref, w_ref, b_ref, o_ref):
    s = _silu(c_ref[...])
    o_ref[...] = jnp.dot(s.astype(BF16), w_ref[...].astype(BF16),
                         preferred_element_type=F32) + b_ref[...]


def _ada_call(cc, ada_w, ada_b):
    d, n = ada_w.shape
    tn = 768
    return pl.pallas_call(
        _ada_kernel,
        grid=(n // tn,),
        in_specs=[pl.BlockSpec((MOD_ROWS, d), lambda j: (0, 0)),
                  pl.BlockSpec((d, tn), lambda j: (0, j)),
                  pl.BlockSpec((1, tn), lambda j: (0, j))],
        out_specs=pl.BlockSpec((MOD_ROWS, tn), lambda j: (0, j)),
        out_shape=jax.ShapeDtypeStruct((MOD_ROWS, n), F32),
        compiler_params=pltpu.CompilerParams(dimension_semantics=("arbitrary",),
                                             vmem_limit_bytes=VMEM_LIMIT),
        name="ada",
    )(cc, ada_w, ada_b.reshape(1, n))


def _mod_rows(mod_b_ref, mod_c_ref, is_ctx, part):
    d = D_MODEL
    mb = mod_b_ref[:, part * d:(part + 1) * d]
    mc = mod_c_ref[:, part * d:(part + 1) * d]
    return jnp.where(is_ctx, mc, mb)


def _prologue(h_ref, mod_b_ref, mod_c_ref, preg_ref, xn_ref, row0, seq):
    tm = h_ref.shape[0]
    preg = preg_ref[...]
    for c in range(tm // ROW_CHUNK):
        rows = pl.ds(c * ROW_CHUNK, ROW_CHUNK)
        is_ctx = (row0 + c * ROW_CHUNK) >= seq
        shift = _mod_rows(mod_b_ref, mod_c_ref, is_ctx, 0)
        scale = _mod_rows(mod_b_ref, mod_c_ref, is_ctx, 1)
        y = _rms(h_ref[rows, :]) * preg
        xn_ref[rows, :] = (y * (1.0 + scale) + shift).astype(BF16)


def _proj_kernel(*refs, kinds, has_v, seq, tiles_per_batch, q_scale, rope_half):
    (h_ref, mod_b_ref, mod_c_ref, preg_ref, w_ref, cos_ref, sa_ref, sb_ref,
     ng0_ref, ng1_ref) = refs[:10]
    if has_v:
        o_ref, vt_ref, g_ref, xn_ref = refs[10:]
    else:
        o_ref, g_ref, xn_ref = refs[10:]
    i = pl.program_id(0)
    j = pl.program_id(1)
    tm = h_ref.shape[0]
    tn = w_ref.shape[1]

    @pl.when(j == 0)
    def _():
        _prologue(h_ref, mod_b_ref, mod_c_ref, preg_ref, xn_ref,
                  (i % tiles_per_batch) * tm, seq)

    def acc():
        return jnp.dot(xn_ref[...], w_ref[...], preferred_element_type=F32)

    def rope(x):
        return _rope(x, cos_ref[...], sa_ref[...], sb_ref[...], rope_half)

    def chunks(a):
        return [a[:, c * LANES:(c + 1) * LANES] for c in range(a.shape[1] // LANES)]

    def store(parts):
        for c, p in enumerate(parts):
            o_ref[:, c * LANES:(c + 1) * LANES] = p.astype(BF16)

    def tile(kind):
        a = acc()
        if kind == "g":
            g_ref[...] = a
        elif kind == "v":
            vt_ref[...] = a.T.astype(BF16)
        elif kind == "da_q":
            store([rope(x) * q_scale for x in chunks(a)])
        elif kind == "da_k":
            store([rope(x) for x in chunks(a)])
        elif kind == "gq_q":
            store([rope(_rms(x) * ng0_ref[...]) * q_scale for x in chunks(a)])
        elif kind == "gq_kv":
            ks = chunks(a[:, :GQ_KV_W])
            store([rope(_rms(x) * ng1_ref[...]) for x in ks]
                  + [jnp.zeros_like(ks[0])] * (tn // LANES - len(ks)))
            vt_ref[...] = a[:, GQ_KV_W:].T.astype(BF16)
        elif kind == "ml_qa":
            o_ref[...] = (_rms(a) * ng0_ref[...]).astype(BF16)
        elif kind == "ml_kva":
            o_ref[...] = (_rms(a) * ng1_ref[...]).astype(BF16)
        elif kind == "ml_kr":
            cs = chunks(a)
            store([rope(cs[0])] + cs[1:])
        else:
            raise ValueError(kind)

    for kind in sorted(set(kinds)):
        js = [t for t, k in enumerate(kinds) if k == kind]
        lo, hi = js[0], js[-1]
        assert js == list(range(lo, hi + 1))
        pl.when((j >= lo) & (j <= hi))(functools.partial(tile, kind))


def _proj_call(h2, mod3, pre_g, w, tables, ng0, ng1, *, kinds, tn, v_rows, batch, seq, ctx_len,
               q_scale, rope_half):
    m, d = h2.shape
    n = w.shape[1]
    t = seq + ctx_len
    tm = PROJ_TM
    tpb = t // tm
    assert t % tm == 0 and n % tn == 0 and len(kinds) == n // tn
    n_g = sum(k == "g" for k in kinds)
    n_v = sum(k in ("v", "gq_kv") for k in kinds)
    n_o = sum(k not in ("v", "g") for k in kinds)
    j_g = len(kinds) - n_g
    j_v = j_g - n_v
    assert all(k == "g" for k in kinds[j_g:]) and n_g * tn == d
    assert all(k in ("v", "gq_kv") for k in kinds[j_v:j_g]) and (n_v > 0) == (v_rows > 0)
    cos, sa, sb = tables
    kern = functools.partial(_proj_kernel, kinds=tuple(kinds), has_v=n_v > 0, seq=seq,
                             tiles_per_batch=tpb, q_scale=q_scale, rope_half=rope_half)
    tab_spec = pl.BlockSpec((tm, LANES), lambda i, j: (i % tpb, 0))
    out_specs = [pl.BlockSpec((tm, tn), lambda i, j: (i, jnp.minimum(j, n_o - 1)))]
    out_shape = [jax.ShapeDtypeStruct((m, n_o * tn), BF16)]
    if n_v:
        out_specs.append(pl.BlockSpec((v_rows, tm),
                                      lambda i, j: (jnp.clip(j - j_v, 0, n_v - 1), i)))
        out_shape.append(jax.ShapeDtypeStruct((n_v * v_rows, m), BF16))
    out_specs.append(pl.BlockSpec((tm, tn), lambda i, j: (i, jnp.maximum(j - j_g, 0))))
    out_shape.append(jax.ShapeDtypeStruct((m, d), F32))
    return pl.pallas_call(
        kern,
        grid=(m // tm, n // tn),
        in_specs=[
            pl.BlockSpec((tm, d), lambda i, j: (i, 0)),
            pl.BlockSpec((None, 1, 3 * d), lambda i, j: (i // tpb, 0, 0)),
            pl.BlockSpec((None, 1, 3 * d), lambda i, j: (batch, 0, 0)),
            pl.BlockSpec((1, d), lambda i, j: (0, 0)),
            pl.BlockSpec((d, tn), lambda i, j: (0, j)),
            tab_spec, tab_spec, tab_spec,
            pl.BlockSpec(ng0.shape, lambda i, j: (0, 0)),
            pl.BlockSpec(ng1.shape, lambda i, j: (0, 0)),
        ],
        out_specs=out_specs,
        out_shape=out_shape,
        scratch_shapes=[pltpu.VMEM((tm, d), BF16)],
        compiler_params=pltpu.CompilerParams(dimension_semantics=("arbitrary", "arbitrary"),
                                             vmem_limit_bytes=VMEM_LIMIT),
        name="proj",
    )(h2, mod3, mod3, pre_g.reshape(1, d), w, cos, sa, sb, ng0, ng1)


def _mla_b_kernel(qa_ref, kva_ref, kr_ref, wq_ref, wkn_ref, wv_ref, cos_ref, sa_ref, sb_ref,
                  q_ref, k_ref, vt_ref, *, q_scale):
    q = jnp.dot(qa_ref[...], wq_ref[...], preferred_element_type=F32)
    for c in range(q.shape[1] // LANES):
        x = q[:, c * LANES:(c + 1) * LANES]
        if c % 2 == 1:
            x = _rope(x, cos_ref[...], sa_ref[...], sb_ref[...], ML_ROPE_DIM // 4)
        q_ref[:, c * LANES:(c + 1) * LANES] = (x * q_scale).astype(BF16)
    kva = kva_ref[...]
    kn = jnp.dot(kva, wkn_ref[...], preferred_element_type=F32)
    kr = kr_ref[...]
    for hh in range(kn.shape[1] // LANES):
        k_ref[:, 2 * hh * LANES:(2 * hh + 1) * LANES] = kn[:, hh * LANES:(hh + 1) * LANES].astype(BF16)
        k_ref[:, (2 * hh + 1) * LANES:(2 * hh + 2) * LANES] = kr
    vt_ref[...] = jnp.dot(kva, wv_ref[...], preferred_element_type=F32).T.astype(BF16)


def _mla_b_call(qkv, wq, wkv, tables, *, seq, ctx_len, q_scale):
    m = qkv.shape[0]
    tm = PROJ_TM
    tpb = (seq + ctx_len) // tm
    r = ML_Q_RANK
    nq = wq.shape[1]
    nkv = wkv.shape[1] // 2
    steps = 4
    tq, tk = nq // steps, nkv // steps
    kr_col = (ML_Q_RANK + ML_KV_RANK) // LANES
    cos, sa, sb = tables
    tab_spec = pl.BlockSpec((tm, LANES), lambda i, j: (i % tpb, 0))
    return pl.pallas_call(
        functools.partial(_mla_b_kernel, q_scale=q_scale),
        grid=(m // tm, steps),
        in_specs=[
            pl.BlockSpec((tm, r), lambda i, j: (i, 0)),
            pl.BlockSpec((tm, r), lambda i, j: (i, 1)),
            pl.BlockSpec((tm, LANES), lambda i, j: (i, kr_col)),
            pl.BlockSpec((r, tq), lambda i, j: (0, j)),
            pl.BlockSpec((r, tk), lambda i, j: (0, j)),
            pl.BlockSpec((r, tk), lambda i, j: (0, steps + j)),
            tab_spec, tab_spec, tab_spec,
        ],
        out_specs=[
            pl.BlockSpec((tm, tq), lambda i, j: (i, j)),
            pl.BlockSpec((tm, 2 * tk), lambda i, j: (i, j)),
            pl.BlockSpec((tk, tm), lambda i, j: (j, i)),
        ],
        out_shape=[jax.ShapeDtypeStruct((m, nq), BF16),
                   jax.ShapeDtypeStruct((m, 2 * nkv), BF16),
                   jax.ShapeDtypeStruct((nkv, m), BF16)],
        compiler_params=pltpu.CompilerParams(dimension_semantics=("arbitrary", "arbitrary"),
                                             vmem_limit_bytes=VMEM_LIMIT),
        name="mla_b",
    )(qkv, qkv, qkv, wq, wkv, wkv, cos, sa, sb)


def _run_chains(chains):
    n = len(chains)
    st, e, l = {}, {}, {}
    for step in range(n + 2):
        if step < n:
            q, k, _, _ = chains[step]
            st[step] = _nt_dot(k(), q())
        i = step - 1
        if 0 <= i < n:
            s = st.pop(i)
            p = jnp.exp2(s - jnp.max(s, axis=0, keepdims=True))
            l[i] = jnp.sum(p, axis=0, keepdims=True)
            e[i] = p.astype(BF16)
        i = step - 2
        if 0 <= i < n:
            _, _, vt, done = chains[i]
            done(jnp.dot(vt(), e.pop(i), preferred_element_type=F32) / l.pop(i))


def _row_blocks(rows_q, seq):
    assert seq % MXU_N == 0
    blocks = [(r0, MXU_N, 0) for r0 in range(0, seq, MXU_N)]
    if rows_q > seq:
        blocks.append((seq, rows_q - seq, seq))
    return blocks


def _diff_attn_kernel(lq1_ref, lk1_ref, lq2_ref, lk2_ref, subg_ref, q_ref, k_ref, vt_ref, g_ref,
                      o_ref, *, seq, lambda_init):
    lam = (jnp.exp(jnp.sum(lq1_ref[...] * lk1_ref[...], axis=-1, keepdims=True))
           - jnp.exp(jnp.sum(lq2_ref[...] * lk2_ref[...], axis=-1, keepdims=True))
           + lambda_init)
    chains = []
    for r0, nr, key0 in _row_blocks(q_ref.shape[0], seq):
        rows = slice(r0, r0 + nr)
        parts = []

        def q_sub(sub, rows=rows):
            qf = q_ref[rows, :].astype(F32)
            lane = lax.broadcasted_iota(jnp.int32, qf.shape, 1)
            return jnp.where((lane >= DA_HEAD_DIM) == bool(sub), qf, 0.0).astype(BF16)

        def done(ot, rows=rows, parts=parts):
            parts.append(ot)
            if len(parts) == 2:
                o = (parts[0] - lam * parts[1]).T
                o = _rms(o) * subg_ref[...] * (1.0 - lambda_init)
                o_ref[rows, :] = (o * _silu(g_ref[rows, :])).astype(BF16)

        for sub in range(2):
            chains.append((functools.partial(q_sub, sub),
                           lambda key0=key0: k_ref[key0:, :],
                           lambda key0=key0: vt_ref[:, key0:], done))
    _run_chains(chains)


def _plain_attn_kernel(q_ref, k_ref, vt_ref, g_ref, o_ref, *, seq, dq, shared_kv):
    n_heads = q_ref.shape[1] // dq
    chains = []
    for r0, nr, key0 in _row_blocks(q_ref.shape[0], seq):
        rows = slice(r0, r0 + nr)
        for hh in range(n_heads):
            kh = 0 if shared_kv else hh
            cols = slice(hh * LANES, (hh + 1) * LANES)

            def done(ot, rows=rows, cols=cols):
                o_ref[rows, cols] = (ot.T * _silu(g_ref[rows, cols])).astype(BF16)

            chains.append((lambda rows=rows, hh=hh: q_ref[rows, hh * dq:(hh + 1) * dq],
                           lambda key0=key0, kh=kh: k_ref[key0:, kh * dq:(kh + 1) * dq],
                           lambda key0=key0, kh=kh: vt_ref[kh * LANES:(kh + 1) * LANES, key0:],
                           done))
    _run_chains(chains)


def _attn_call(kind, q3, k3, vt, g3, small, *, seq, need_ctx, lambda_init=None):
    batch, t, _ = q3.shape
    rows = t if need_ctx else seq
    heads = D_MODEL // LANES
    if kind == "diff":
        hps, kv_heads, dq, k_col0 = 1, 1, LANES, heads
        kern = functools.partial(_diff_attn_kernel, seq=seq, lambda_init=lambda_init)
    elif kind == "gqa":
        hps, kv_heads, dq, k_col0 = GQ_HEADS // GQ_KV_HEADS, 1, LANES, GQ_HEADS
        kern = functools.partial(_plain_attn_kernel, seq=seq, dq=dq, shared_kv=True)
    elif kind == "mla":
        hps, kv_heads, dq, k_col0 = MLA_HEADS_PER_STEP, MLA_HEADS_PER_STEP, 2 * LANES, 0
        kern = functools.partial(_plain_attn_kernel, seq=seq, dq=dq, shared_kv=False)
    else:
        raise ValueError(kind)
    kw = kv_heads * dq
    assert (k_col0 * dq) % kw == 0
    in_specs = [pl.BlockSpec(a.shape, lambda b, s: (0,) * a.ndim) for a in small] + [
        pl.BlockSpec((None, rows, hps * dq), lambda b, s: (b, 0, s)),
        pl.BlockSpec((None, t, kw), lambda b, s: (b, 0, k_col0 * dq // kw + s)),
        pl.BlockSpec((kv_heads * LANES, t), lambda b, s: (s, b)),
        pl.BlockSpec((None, rows, hps * LANES), lambda b, s: (b, 0, s)),
    ]
    return pl.pallas_call(
        kern,
        grid=(batch, heads // hps),
        in_specs=in_specs,
        out_specs=pl.BlockSpec((None, rows, hps * LANES), lambda b, s: (b, 0, s)),
        out_shape=jax.ShapeDtypeStruct((batch, rows, D_MODEL), BF16),
        compiler_params=pltpu.CompilerParams(dimension_semantics=("arbitrary", "arbitrary"),
                                             vmem_limit_bytes=VMEM_LIMIT),
        name="attn_" + kind,
    )(*small, q3, k3, vt, g3)


def _finish_kernel(a_ref, h_ref, mod_b_ref, mod_c_ref, postg_ref, w_ref, o_ref, *, seq):
    ti = pl.program_id(1)
    tm = a_ref.shape[0]
    y = jnp.dot(a_ref[...], w_ref[...], preferred_element_type=F32)
    postg = postg_ref[...]
    for c in range(tm // ROW_CHUNK):
        rows = slice(c * ROW_CHUNK, (c + 1) * ROW_CHUNK)
        is_ctx = (ti * tm + c * ROW_CHUNK) >= seq
        gate = _mod_rows(mod_b_ref, mod_c_ref, is_ctx, 2)
        o_ref[rows, :] = h_ref[rows, :] + gate * (_rms(y[rows, :]) * postg)


def _finish_call(a3, h3, mod3, post_g, w, *, seq):
    batch, t, d = h3.shape
    rows_out = a3.shape[1]
    tm = FINISH_TM if rows_out == t else FINISH_TM_LAST
    assert rows_out % tm == 0
    blk = pl.BlockSpec((None, tm, d), lambda b, i: (b, i, 0))
    return pl.pallas_call(
        functools.partial(_finish_kernel, seq=seq),
        grid=(batch, rows_out // tm),
        in_specs=[
            blk, blk,
            pl.BlockSpec((None, 1, 3 * d), lambda b, i: (b, 0, 0)),
            pl.BlockSpec((None, 1, 3 * d), lambda b, i: (batch, 0, 0)),
            pl.BlockSpec((1, d), lambda b, i: (0, 0)),
            pl.BlockSpec((d, d), lambda b, i: (0, 0)),
        ],
        out_specs=blk,
        out_shape=jax.ShapeDtypeStruct((batch, rows_out, d), F32),
        compiler_params=pltpu.CompilerParams(dimension_semantics=("arbitrary", "arbitrary"),
                                             vmem_limit_bytes=VMEM_LIMIT),
        name="finish",
    )(a3, h3, mod3, mod3, post_g.reshape(1, d), w)


def _lambda_init(layer):
    return 0.8 - 0.6 * math.exp(-0.3 * layer)


def _layer(kind, layer, h3, cc, p, need_ctx, tables, *, seq, ctx_len):
    batch, t, d = h3.shape
    m = batch * t
    mod3 = _ada_call(cc, p["ada_w"], p["ada_b"]).reshape(MOD_ROWS, 1, 3 * d)
    h2 = h3.reshape(m, d)
    one = jnp.ones((1, LANES), F32)
    common = dict(batch=batch, seq=seq, ctx_len=ctx_len)

    def b3(a):
        return a.reshape(batch, t, a.shape[-1])

    if kind == "diff":
        qk, vt, g = _proj_call(h2, mod3, p["pre_g"], p["w_in"].astype(BF16), tables[DA_HEAD_DIM],
                               one, one, kinds=["da_q"] * 2 + ["da_k"] * 2 + ["v"] * 2 + ["g"] * 2,
                               tn=1024, v_rows=1024, q_scale=DA_HEAD_DIM ** -0.5 * LOG2_E,
                               rope_half=DA_HEAD_DIM // 4, **common)
        small = [p[n].reshape(1, DA_HEAD_DIM) for n in ("lam_q1", "lam_k1", "lam_q2", "lam_k2")]
        small.append(p["subln_g"].reshape(1, 2 * DA_HEAD_DIM))
        a3 = _attn_call("diff", b3(qk), b3(qk), vt, b3(g), small, seq=seq, need_ctx=need_ctx,
                        lambda_init=_lambda_init(layer))
    elif kind == "gqa":
        qk, vt, g = _proj_call(h2, mod3, p["pre_g"], p["w_in"].astype(BF16), tables[GQ_HEAD_DIM],
                               p["q_norm_g"].reshape(1, GQ_HEAD_DIM),
                               p["k_norm_g"].reshape(1, GQ_HEAD_DIM),
                               kinds=["gq_q"] * 2 + ["gq_kv"] + ["g"] * 2,
                               tn=1024, v_rows=GQ_KV_W, q_scale=GQ_HEAD_DIM ** -0.5 * LOG2_E,
                               rope_half=GQ_HEAD_DIM // 4, **common)
        a3 = _attn_call("gqa", b3(qk), b3(qk), vt, b3(g), [], seq=seq, need_ctx=need_ctx)
    elif kind == "mla":
        w_in = p["w_in"]
        r2 = ML_Q_RANK + ML_KV_RANK
        tn = ML_Q_RANK
        w_pad = jnp.concatenate(
            [w_in[:, :r2 + ML_ROPE_DIM], jnp.zeros((d, tn - ML_ROPE_DIM), F32),
             w_in[:, r2 + ML_ROPE_DIM:]], axis=1).astype(BF16)
        qkv, g = _proj_call(h2, mod3, p["pre_g"], w_pad, tables[ML_ROPE_DIM],
                            p["q_a_norm_g"].reshape(1, ML_Q_RANK),
                            p["kv_a_norm_g"].reshape(1, ML_KV_RANK),
                            kinds=["ml_qa", "ml_kva", "ml_kr"] + ["g"] * (d // tn),
                            tn=tn, v_rows=0, q_scale=1.0, rope_half=ML_ROPE_DIM // 4, **common)
        qd = ML_NOPE_DIM + ML_ROPE_DIM
        wq = p["w_q_b"].reshape(ML_Q_RANK, ML_HEADS, qd)
        wq = jnp.pad(wq, ((0, 0), (0, 0), (0, 2 * LANES - qd)))
        wq = wq.reshape(ML_Q_RANK, ML_HEADS * 2 * LANES).astype(BF16)
        wkv = p["w_kv_b"].reshape(ML_KV_RANK, ML_HEADS, ML_NOPE_DIM + ML_V_DIM)
        wkv = jnp.concatenate([wkv[:, :, :ML_NOPE_DIM].reshape(ML_KV_RANK, -1),
                               wkv[:, :, ML_NOPE_DIM:].reshape(ML_KV_RANK, -1)],
                              axis=1).astype(BF16)
        q, k, vt = _mla_b_call(qkv, wq, wkv, tables[ML_ROPE_DIM], seq=seq, ctx_len=ctx_len,
                               q_scale=qd ** -0.5 * LOG2_E)
        a3 = _attn_call("mla", b3(q), b3(k), vt, b3(g), [], seq=seq, need_ctx=need_ctx)
    else:
        raise ValueError(kind)
    return _finish_call(a3, h3, mod3, p["post_g"], p["w_out"].astype(BF16), seq=seq)


def kernel(x, c, ctx, c_ctx, l0_ada_w, l0_ada_b, l0_pre_g, l0_post_g, l0_w_in, l0_lam_q1, l0_lam_k1, l0_lam_q2, l0_lam_k2, l0_subln_g, l0_w_out, l1_ada_w, l1_ada_b, l1_pre_g, l1_post_g, l1_w_in, l1_q_norm_g, l1_k_norm_g, l1_w_out, l2_ada_w, l2_ada_b, l2_pre_g, l2_post_g, l2_w_in, l2_q_a_norm_g, l2_w_q_b, l2_kv_a_norm_g, l2_w_kv_b, l2_w_out, l3_ada_w, l3_ada_b, l3_pre_g, l3_post_g, l3_w_in, l3_lam_q1, l3_lam_k1, l3_lam_q2, l3_lam_k2, l3_subln_g, l3_w_out):
    batch, seq, d = x.shape
    ctx_len = ctx.shape[1]
    assert d == D_MODEL and seq % GRID_W == 0 and batch + 1 <= MOD_ROWS
    diff_names = ("ada_w", "ada_b", "pre_g", "post_g", "w_in", "lam_q1", "lam_k1", "lam_q2",
                  "lam_k2", "subln_g", "w_out")
    layers = [
        ("diff", dict(zip(diff_names, (l0_ada_w, l0_ada_b, l0_pre_g, l0_post_g, l0_w_in,
                                       l0_lam_q1, l0_lam_k1, l0_lam_q2, l0_lam_k2, l0_subln_g,
                                       l0_w_out)))),
        ("gqa", dict(ada_w=l1_ada_w, ada_b=l1_ada_b, pre_g=l1_pre_g, post_g=l1_post_g,
                     w_in=l1_w_in, q_norm_g=l1_q_norm_g, k_norm_g=l1_k_norm_g, w_out=l1_w_out)),
        ("mla", dict(ada_w=l2_ada_w, ada_b=l2_ada_b, pre_g=l2_pre_g, post_g=l2_post_g,
                     w_in=l2_w_in, q_a_norm_g=l2_q_a_norm_g, w_q_b=l2_w_q_b,
                     kv_a_norm_g=l2_kv_a_norm_g, w_kv_b=l2_w_kv_b, w_out=l2_w_out)),
        ("diff", dict(zip(diff_names, (l3_ada_w, l3_ada_b, l3_pre_g, l3_post_g, l3_w_in,
                                       l3_lam_q1, l3_lam_k1, l3_lam_q2, l3_lam_k2, l3_subln_g,
                                       l3_w_out)))),
    ]
    cc = jnp.zeros((MOD_ROWS, d), F32).at[:batch].set(c).at[batch].set(c_ctx)
    tables = {hd: _rope_tables(seq, ctx_len, hd) for hd in (DA_HEAD_DIM, GQ_HEAD_DIM)}
    h3 = jnp.concatenate([x, ctx], axis=1)
    for layer, (kind, p) in enumerate(layers):
        h3 = _layer(kind, layer, h3, cc, p, layer < len(layers) - 1, tables,
                    seq=seq, ctx_len=ctx_len)
    return h3
```

```python
import functools
import math

import jax
import jax.numpy as jnp
from jax import lax
from jax.experimental import pallas as pl
from jax.experimental.pallas import tpu as pltpu

F32 = jnp.float32
BF16 = jnp.bfloat16
LOG2_E = math.log2(math.e)

D_MODEL = 2048
GRID_W = 64
ROPE_BASE = 10000.0
NORM_EPS = 1e-6
DA_HEAD_DIM = 64
DA_HEADS = D_MODEL // (2 * DA_HEAD_DIM)
GQ_HEAD_DIM = 128
GQ_HEADS = D_MODEL // GQ_HEAD_DIM
GQ_KV_HEADS = GQ_HEADS // 4
GQ_KV_W = GQ_KV_HEADS * GQ_HEAD_DIM
ML_NOPE_DIM = 128
ML_ROPE_DIM = 64
ML_V_DIM = 128
ML_HEADS = D_MODEL // ML_V_DIM
ML_Q_RANK = D_MODEL // 4
ML_KV_RANK = D_MODEL // 4

LANES = 128
SUBLANES = 8
MXU_N = 512
ROW_CHUNK = 128
VMEM_LIMIT = 56 * 1024 * 1024
MOD_ROWS = 16

PROJ_TM = 768
FINISH_TM = 384
FINISH_TM_LAST = 512
MLA_HEADS_PER_STEP = 2
KEY_CHUNK = 256


def _silu(x):
    return x / (1.0 + jnp.exp(-x))


def _rms(x, eps=NORM_EPS):
    return x * lax.rsqrt(jnp.mean(x * x, axis=-1, keepdims=True) + eps)


def _rope(x, cos, sin_a, sin_b, half):
    return (x * cos + pltpu.roll(x, LANES - half, 1) * sin_a + pltpu.roll(x, half, 1) * sin_b)


def _swap_halves(x, half):
    parts = []
    for r in range(0, x.shape[0], 2 * half):
        parts += [x[r + half:r + 2 * half], x[r:r + half]]
    return jnp.concatenate(parts, axis=0)


def _rope_t(x, cos_t, sin_t, half):
    return x * cos_t + _swap_halves(x, half) * sin_t


def _rope_tables(seq, ctx_len, head_dim):
    t = seq + ctx_len
    r = jnp.arange(t)
    lat = (r < seq)[:, None]
    rowpos = (r // GRID_W).astype(F32)[:, None]
    colpos = (r % GRID_W).astype(F32)[:, None]
    lane = jnp.arange(LANES)
    u = lane % head_dim
    half2 = head_dim // 2
    half = half2 // 2
    grp = (u // half2)[None, :]
    w = u % half2
    first = (w < half)[None, :]
    inv_freq = (ROPE_BASE ** (-(w % half).astype(F32) / half))[None, :]
    ang = jnp.where(grp == 0, rowpos, colpos) * inv_freq
    cos = jnp.where(lat, jnp.cos(ang), 1.0)
    sin = jnp.where(lat, jnp.sin(ang), 0.0)
    sin_a = jnp.where(first, -sin, 0.0)
    sin_b = jnp.where(first, 0.0, sin)
    return cos.astype(F32), sin_a.astype(F32), sin_b.astype(F32)


def _rope_tables_t(seq, ctx_len, head_dim, gain, scale):
    cos, sin_a, sin_b = _rope_tables(seq, ctx_len, head_dim)
    half = head_dim // 4
    lane = jnp.arange(LANES)
    first = (lane % (2 * half)) < half
    partner = jnp.where(first, lane + half, lane - half)
    g = jnp.ones((LANES,), F32) if gain is None else jnp.tile(gain.astype(F32), LANES // head_dim)
    cos_t = (cos * (g * scale)[None, :]).T
    sin_t = ((sin_a + sin_b) * (g[partner] * scale)[None, :]).T
    return cos_t, sin_t


def _ada_kernel(c_ref, w_ref, b_ref, o_ref):
    s = _silu(c_ref[...])
    o_ref[...] = jnp.dot(s.astype(BF16), w_ref[...].astype(BF16),
                         preferred_element_type=F32) + b_ref[...]


def _ada_call(cc, ada_w, ada_b):
    d, n = ada_w.shape
    tn = 768
    return pl.pallas_call(
        _ada_kernel,
        grid=(n // tn,),
        in_specs=[pl.BlockSpec((MOD_ROWS, d), lambda j: (0, 0)),
                  pl.BlockSpec((d, tn), lambda j: (0, j)),
                  pl.BlockSpec((1, tn), lambda j: (0, j))],
        out_specs=pl.BlockSpec((MOD_ROWS, tn), lambda j: (0, j)),
        out_shape=jax.ShapeDtypeStruct((MOD_ROWS, n), F32),
        compiler_params=pltpu.CompilerParams(dimension_semantics=("arbitrary",),
                                             vmem_limit_bytes=VMEM_LIMIT),
        name="ada",
    )(cc, ada_w, ada_b.reshape(1, n))


_KIND_OUTS = {"da_q": ("qt",), "gq_q": ("qt",), "da_k": ("o",), "gq_kv": ("o", "vt"),
              "v": ("vt",), "g": ("g",), "ml_qa": ("o",), "ml_kva": ("o",), "ml_kr": ("o",)}


def _mod_rows(mod_b_ref, mod_c_ref, is_ctx, part):
    d = D_MODEL
    mb = mod_b_ref[:, part * d:(part + 1) * d]
    mc = mod_c_ref[:, part * d:(part + 1) * d]
    return jnp.where(is_ctx, mc, mb)


def _prologue(h_ref, mod_b_ref, mod_c_ref, preg_ref, xn_ref, slot, row0, seq, pieces):
    preg = preg_ref[...]
    for c in pieces:
        rows = pl.ds(c * ROW_CHUNK, ROW_CHUNK)
        is_ctx = (row0 + c * ROW_CHUNK) >= seq
        shift = _mod_rows(mod_b_ref, mod_c_ref, is_ctx, 0)
        scale = _mod_rows(mod_b_ref, mod_c_ref, is_ctx, 1)
        y = _rms(h_ref[rows, :]) * preg
        xn_ref[slot, rows, :] = (y * (1.0 + scale) + shift).astype(BF16)


def _proj_kernel(*refs, kinds, out_names, n_tabs, seq, n_tiles, tiles_per_batch, rope_half):
    h_ref, mod_b_ref, mod_c_ref, preg_ref, w_ref = refs[:5]
    tabs = refs[5:5 + n_tabs]
    ng0_ref, ng1_ref = refs[5 + n_tabs:7 + n_tabs]
    outs = dict(zip(out_names, refs[7 + n_tabs:]))
    xn_ref = refs[-1]
    i = pl.program_id(0)
    j = pl.program_id(1)
    tm = h_ref.shape[0]
    n_steps = len(kinds)
    n_pieces = tm // ROW_CHUNK
    row0 = (jnp.minimum(i, n_tiles - 1) % tiles_per_batch) * tm
    prologue = functools.partial(_prologue, h_ref, mod_b_ref, mod_c_ref, preg_ref, xn_ref,
                                 i % 2, row0, seq)

    @pl.when((i == 0) & (j == 0))
    def _():
        prologue(range(n_pieces))

    def heads_t(a):
        at = a.T
        return [at[c * LANES:(c + 1) * LANES] for c in range(at.shape[0] // LANES)]

    def head_rms(xh):
        return lax.rsqrt(jnp.mean(xh * xh, axis=0, keepdims=True) + NORM_EPS)

    def tile(kind):
        a = jnp.dot(xn_ref[(i + 1) % 2], w_ref[...], preferred_element_type=F32)
        if kind == "g":
            outs["g"][...] = a
        elif kind == "v":
            outs["vt"][...] = a.T.astype(BF16)
        elif kind in ("da_q", "gq_q"):
            cos_t, sin_t = tabs[0][...], tabs[1][...]
            for c, xh in enumerate(heads_t(a)):
                y = _rope_t(xh, cos_t, sin_t, rope_half)
                if kind == "gq_q":
                    y = y * head_rms(xh)
                outs["qt"][c * LANES:(c + 1) * LANES, :] = y.astype(BF16)
        elif kind in ("da_k", "gq_kv"):
            cos_t, sin_t = tabs[2][...], tabs[3][...]
            k_cols = outs["o"].shape[1]
            for c, xh in enumerate(heads_t(a[:, :k_cols])):
                y = _rope_t(xh, cos_t, sin_t, rope_half)
                if kind == "gq_kv":
                    y = y * head_rms(xh)
                outs["o"][:, c * LANES:(c + 1) * LANES] = y.T.astype(BF16)
            if kind == "gq_kv":
                outs["vt"][...] = a[:, k_cols:].T.astype(BF16)
        elif kind == "ml_qa":
            outs["o"][...] = (_rms(a) * ng0_ref[...]).astype(BF16)
        elif kind == "ml_kva":
            outs["o"][...] = (_rms(a) * ng1_ref[...]).astype(BF16)
        elif kind == "ml_kr":
            outs["o"][...] = a.astype(BF16)
            outs["o"][:, :LANES] = _rope(a[:, :LANES], tabs[0][...], tabs[1][...], tabs[2][...],
                                         rope_half).astype(BF16)
        else:
            raise ValueError(kind)

    def step(jv):
        tile(kinds[jv])
        prologue([p for p in range(n_pieces) if n_steps - 1 - p % n_steps == jv])

    for jv in range(n_steps):
        pl.when((i > 0) & (j == jv))(functools.partial(step, jv))


def _proj_call(h2, mod3, pre_g, w, tabs, ng0, ng1, *, kinds, tn, o_cols, v_rows, tabs_transposed,
               batch, seq, ctx_len, rope_half):
    m, d = h2.shape
    n = w.shape[1]
    t = seq + ctx_len
    tm = PROJ_TM
    tpb = t // tm
    n_tiles = m // tm
    assert t % tm == 0 and n % tn == 0 and len(kinds) == n // tn

    def nxt(i):
        return jnp.minimum(i, n_tiles - 1)

    def cur(i):
        return jnp.maximum(i - 1, 0)

    def tile_idx(name):
        js = [jv for jv, k in enumerate(kinds) if name in _KIND_OUTS[k]]
        assert js == list(range(js[0], js[-1] + 1))
        return lambda i, j: jnp.where(i == 0, 0, jnp.clip(j - js[0], 0, len(js) - 1)), len(js)

    out_names, out_specs, out_shape = [], [], []
    for name in ("qt", "o", "vt", "g"):
        if not any(name in _KIND_OUTS[k] for k in kinds):
            continue
        idx, cnt = tile_idx(name)
        out_names.append(name)
        if name in ("qt", "vt"):
            rows = tn if name == "qt" else v_rows
            out_specs.append(pl.BlockSpec((rows, tm), lambda i, j, idx=idx: (idx(i, j), cur(i))))
            out_shape.append(jax.ShapeDtypeStruct((cnt * rows, m), BF16))
        else:
            cols, dt = (o_cols, BF16) if name == "o" else (tn, F32)
            out_specs.append(pl.BlockSpec((tm, cols), lambda i, j, idx=idx: (cur(i), idx(i, j))))
            out_shape.append(jax.ShapeDtypeStruct((m, cnt * cols), dt))
    assert out_shape[-1].shape == (m, d)

    if tabs_transposed:
        tab_spec = pl.BlockSpec((LANES, tm), lambda i, j: (0, cur(i) % tpb))
    else:
        tab_spec = pl.BlockSpec((tm, LANES), lambda i, j: (cur(i) % tpb, 0))
    kern = functools.partial(_proj_kernel, kinds=tuple(kinds), out_names=tuple(out_names),
                             n_tabs=len(tabs), seq=seq, n_tiles=n_tiles, tiles_per_batch=tpb,
                             rope_half=rope_half)
    return pl.pallas_call(
        kern,
        grid=(n_tiles + 1, n // tn),
        in_specs=[
            pl.BlockSpec((tm, d), lambda i, j: (nxt(i), 0)),
            pl.BlockSpec((None, 1, 3 * d), lambda i, j: (nxt(i) // tpb, 0, 0)),
            pl.BlockSpec((None, 1, 3 * d), lambda i, j: (batch, 0, 0)),
            pl.BlockSpec((1, d), lambda i, j: (0, 0)),
            pl.BlockSpec((d, tn), lambda i, j: (0, jnp.where(i == 0, 0, j))),
        ] + [tab_spec] * len(tabs) + [
            pl.BlockSpec(ng0.shape, lambda i, j: (0, 0)),
            pl.BlockSpec(ng1.shape, lambda i, j: (0, 0)),
        ],
        out_specs=out_specs,
        out_shape=out_shape,
        scratch_shapes=[pltpu.VMEM((2, tm, d), BF16)],
        compiler_params=pltpu.CompilerParams(dimension_semantics=("arbitrary", "arbitrary"),
                                             vmem_limit_bytes=VMEM_LIMIT),
        name="proj",
    )(h2, mod3, mod3, pre_g.reshape(1, d), w, *tabs, ng0, ng1)


def _mla_b_kernel(qa_ref, kva_ref, kr_ref, wq_ref, wkn_ref, wv_ref, cos_t_ref, sin_t_ref,
                  qt_ref, k_ref, vt_ref, *, q_scale):
    qt = jnp.dot(qa_ref[...], wq_ref[...], preferred_element_type=F32).T
    for c in range(qt.shape[0] // LANES):
        x = qt[c * LANES:(c + 1) * LANES]
        if c % 2 == 1:
            x = _rope_t(x, cos_t_ref[...], sin_t_ref[...], ML_ROPE_DIM // 4)
        else:
            x = x * q_scale
        qt_ref[c * LANES:(c + 1) * LANES, :] = x.astype(BF16)
    kva = kva_ref[...]
    kn = jnp.dot(kva, wkn_ref[...], preferred_element_type=F32)
    kr = kr_ref[...]
    for hh in range(kn.shape[1] // LANES):
        k_ref[:, 2 * hh * LANES:(2 * hh + 1) * LANES] = kn[:, hh * LANES:(hh + 1) * LANES].astype(BF16)
        k_ref[:, (2 * hh + 1) * LANES:(2 * hh + 2) * LANES] = kr
    vt_ref[...] = jnp.dot(kva, wv_ref[...], preferred_element_type=F32).T.astype(BF16)


def _mla_b_call(qkv, wq, wkv, tabs_t, *, seq, ctx_len, q_scale):
    m = qkv.shape[0]
    tm = PROJ_TM
    tpb = (seq + ctx_len) // tm
    r = ML_Q_RANK
    nq = wq.shape[1]
    nkv = wkv.shape[1] // 2
    steps = 4
    tq, tk = nq // steps, nkv // steps
    kr_col = (ML_Q_RANK + ML_KV_RANK) // LANES
    tab_spec = pl.BlockSpec((LANES, tm), lambda i, j: (0, i % tpb))
    return pl.pallas_call(
        functools.partial(_mla_b_kernel, q_scale=q_scale),
        grid=(m // tm, steps),
        in_specs=[
            pl.BlockSpec((tm, r), lambda i, j: (i, 0)),
            pl.BlockSpec((tm, r), lambda i, j: (i, 1)),
            pl.BlockSpec((tm, LANES), lambda i, j: (i, kr_col)),
            pl.BlockSpec((r, tq), lambda i, j: (0, j)),
            pl.BlockSpec((r, tk), lambda i, j: (0, j)),
            pl.BlockSpec((r, tk), lambda i, j: (0, steps + j)),
            tab_spec, tab_spec,
        ],
        out_specs=[
            pl.BlockSpec((tq, tm), lambda i, j: (j, i)),
            pl.BlockSpec((tm, 2 * tk), lambda i, j: (i, j)),
            pl.BlockSpec((tk, tm), lambda i, j: (j, i)),
        ],
        out_shape=[jax.ShapeDtypeStruct((nq, m), BF16),
                   jax.ShapeDtypeStruct((m, 2 * nkv), BF16),
                   jax.ShapeDtypeStruct((nkv, m), BF16)],
        compiler_params=pltpu.CompilerParams(dimension_semantics=("arbitrary", "arbitrary"),
                                             vmem_limit_bytes=VMEM_LIMIT),
        name="mla_b",
    )(qkv, qkv, qkv, wq, wkv, wkv, *tabs_t)


def _sublane_groups(x):
    return x.reshape(x.shape[0] // SUBLANES, SUBLANES, x.shape[1])


def _run_chains(chains, s_scr, e_scr):
    n = len(chains)
    shape, m, l = {}, {}, {}
    for step in range(n + 2):
        if step < n:
            qt, k, vt, _ = chains[step]
            qv = qt()
            nq, nk = qv.shape[1], vt().shape[1]
            shape[step] = (nk, nq)
            m8 = None
            for c0 in range(0, nk, KEY_CHUNK):
                st = jnp.dot(k(c0, KEY_CHUNK), qv, preferred_element_type=F32)
                s_scr[step % 2, c0:c0 + KEY_CHUNK, :nq] = st
                cm = jnp.max(_sublane_groups(st), axis=0)
                m8 = cm if m8 is None else jnp.maximum(m8, cm)
            m[step] = jnp.max(m8, axis=0, keepdims=True)
        i = step - 1
        if 0 <= i < n:
            nk, nq = shape[i]
            mi = m.pop(i)
            l8 = None
            for c0 in range(0, nk, KEY_CHUNK):
                p = jnp.exp2(s_scr[i % 2, c0:c0 + KEY_CHUNK, :nq] - mi)
                ps = jnp.sum(_sublane_groups(p), axis=0)
                l8 = ps if l8 is None else l8 + ps
                e_scr[i % 2, c0:c0 + KEY_CHUNK, :nq] = p.astype(BF16)
            l[i] = jnp.sum(l8, axis=0, keepdims=True)
        i = step - 2
        if 0 <= i < n:
            nk, nq = shape[i]
            _, _, vt, done = chains[i]
            done(jnp.dot(vt(), e_scr[i % 2, :nk, :nq], preferred_element_type=F32) / l.pop(i))


def _row_blocks(rows_q, seq):
    assert seq % MXU_N == 0
    blocks = [(r0, MXU_N, 0) for r0 in range(0, seq, MXU_N)]
    if rows_q > seq:
        blocks.append((seq, rows_q - seq, seq))
    return blocks


def _diff_attn_kernel(lq1_ref, lk1_ref, lq2_ref, lk2_ref, subg_ref, qt_ref, k_ref, vt_ref, g_ref,
                      o_ref, s_scr, e_scr, *, seq, lambda_init):
    lam = (jnp.exp(jnp.sum(lq1_ref[...] * lk1_ref[...], axis=-1, keepdims=True))
           - jnp.exp(jnp.sum(lq2_ref[...] * lk2_ref[...], axis=-1, keepdims=True))
           + lambda_init)
    chains = []
    for r0, nr, key0 in _row_blocks(o_ref.shape[0], seq):
        rows = slice(r0, r0 + nr)
        parts = []

        def q_sub(sub, rows=rows):
            qf = qt_ref[:, rows].astype(F32)
            row = lax.broadcasted_iota(jnp.int32, qf.shape, 0)
            return jnp.where((row >= DA_HEAD_DIM) == bool(sub), qf, 0.0).astype(BF16)

        def done(ot, rows=rows, parts=parts):
            parts.append(ot)
            if len(parts) == 2:
                o = (parts[0] - lam * parts[1]).T
                o = _rms(o) * subg_ref[...] * (1.0 - lambda_init)
                o_ref[rows, :] = (o * _silu(g_ref[rows, :])).astype(BF16)

        for sub in range(2):
            chains.append((functools.partial(q_sub, sub),
                           lambda c0, kc, key0=key0: k_ref[key0 + c0:key0 + c0 + kc, :],
                           lambda key0=key0: vt_ref[:, key0:], done))
    _run_chains(chains, s_scr, e_scr)


def _plain_attn_kernel(qt_ref, k_ref, vt_ref, g_ref, o_ref, s_scr, e_scr, *, seq, dq, shared_kv):
    n_heads = qt_ref.shape[0] // dq
    chains = []
    for r0, nr, key0 in _row_blocks(o_ref.shape[0], seq):
        rows = slice(r0, r0 + nr)
        for hh in range(n_heads):
            kh = 0 if shared_kv else hh
            cols = slice(hh * LANES, (hh + 1) * LANES)

            def done(ot, rows=rows, cols=cols):
                o_ref[rows, cols] = (ot.T * _silu(g_ref[rows, cols])).astype(BF16)

            chains.append((lambda rows=rows, hh=hh: qt_ref[hh * dq:(hh + 1) * dq, rows],
                           lambda c0, kc, key0=key0, kh=kh: k_ref[key0 + c0:key0 + c0 + kc,
                                                                  kh * dq:(kh + 1) * dq],
                           lambda key0=key0, kh=kh: vt_ref[kh * LANES:(kh + 1) * LANES, key0:],
                           done))
    _run_chains(chains, s_scr, e_scr)


def _attn_call(kind, qt, k3, vt, g3, small, *, seq, need_ctx, lambda_init=None):
    batch, t, _ = k3.shape
    rows = t if need_ctx else seq
    heads = D_MODEL // LANES
    if kind == "diff":
        hps, kv_heads, dq, k_col0 = 1, 1, LANES, 0
        kern = functools.partial(_diff_attn_kernel, seq=seq, lambda_init=lambda_init)
    elif kind == "gqa":
        hps, kv_heads, dq, k_col0 = GQ_HEADS // GQ_KV_HEADS, 1, LANES, 0
        kern = functools.partial(_plain_attn_kernel, seq=seq, dq=dq, shared_kv=True)
    elif kind == "mla":
        hps, kv_heads, dq, k_col0 = MLA_HEADS_PER_STEP, MLA_HEADS_PER_STEP, 2 * LANES, 0
        kern = functools.partial(_plain_attn_kernel, seq=seq, dq=dq, shared_kv=False)
    else:
        raise ValueError(kind)
    kw = kv_heads * dq
    in_specs = [pl.BlockSpec(a.shape, lambda b, s: (0,) * a.ndim) for a in small] + [
        pl.BlockSpec((hps * dq, t), lambda b, s: (s, b)),
        pl.BlockSpec((None, t, kw), lambda b, s: (b, 0, k_col0 + s)),
        pl.BlockSpec((kv_heads * LANES, t), lambda b, s: (s, b)),
        pl.BlockSpec((None, rows, hps * LANES), lambda b, s: (b, 0, s)),
    ]
    return pl.pallas_call(
        kern,
        grid=(batch, heads // hps),
        in_specs=in_specs,
        out_specs=pl.BlockSpec((None, rows, hps * LANES), lambda b, s: (b, 0, s)),
        out_shape=jax.ShapeDtypeStruct((batch, rows, D_MODEL), BF16),
        scratch_shapes=[pltpu.VMEM((2, t, MXU_N), F32), pltpu.VMEM((2, t, MXU_N), BF16)],
        compiler_params=pltpu.CompilerParams(dimension_semantics=("arbitrary", "arbitrary"),
                                             vmem_limit_bytes=VMEM_LIMIT),
        name="attn_" + kind,
    )(*small, qt, k3, vt, g3)


def _finish_kernel(a_ref, h_ref, mod_b_ref, mod_c_ref, postg_ref, w_ref, o_ref, *, seq):
    ti = pl.program_id(1)
    tm = a_ref.shape[0]
    y = jnp.dot(a_ref[...], w_ref[...], preferred_element_type=F32)
    postg = postg_ref[...]
    for c in range(tm // ROW_CHUNK):
        rows = slice(c * ROW_CHUNK, (c + 1) * ROW_CHUNK)
        is_ctx = (ti * tm + c * ROW_CHUNK) >= seq
        gate = _mod_rows(mod_b_ref, mod_c_ref, is_ctx, 2)
        o_ref[rows, :] = h_ref[rows, :] + gate * (_rms(y[rows, :]) * postg)


def _finish_call(a3, h3, mod3, post_g, w, *, seq):
    batch, t, d = h3.shape
    rows_out = a3.shape[1]
    tm = FINISH_TM if rows_out == t else FINISH_TM_LAST
    assert rows_out % tm == 0
    blk = pl.BlockSpec((None, tm, d), lambda b, i: (b, i, 0))
    return pl.pallas_call(
        functools.partial(_finish_kernel, seq=seq),
        grid=(batch, rows_out // tm),
        in_specs=[
            blk, blk,
            pl.BlockSpec((None, 1, 3 * d), lambda b, i: (b, 0, 0)),
            pl.BlockSpec((None, 1, 3 * d), lambda b, i: (batch, 0, 0)),
            pl.BlockSpec((1, d), lambda b, i: (0, 0)),
            pl.BlockSpec((d, d), lambda b, i: (0, 0)),
        ],
        out_specs=blk,
        out_shape=jax.ShapeDtypeStruct((batch, rows_out, d), F32),
        compiler_params=pltpu.CompilerParams(dimension_semantics=("arbitrary", "arbitrary"),
                                             vmem_limit_bytes=VMEM_LIMIT),
        name="finish",
    )(a3, h3, mod3, mod3, post_g.reshape(1, d), w)


def _lambda_init(layer):
    return 0.8 - 0.6 * math.exp(-0.3 * layer)


def _layer(kind, layer, h3, cc, p, need_ctx, *, seq, ctx_len):
    batch, t, d = h3.shape
    m = batch * t
    mod3 = _ada_call(cc, p["ada_w"], p["ada_b"]).reshape(MOD_ROWS, 1, 3 * d)
    h2 = h3.reshape(m, d)
    one = jnp.ones((1, LANES), F32)
    common = dict(batch=batch, seq=seq, ctx_len=ctx_len)
    tables_t = functools.partial(_rope_tables_t, seq, ctx_len)

    def b3(a):
        return a.reshape(batch, t, a.shape[-1])

    if kind == "diff":
        tabs = (tables_t(DA_HEAD_DIM, None, DA_HEAD_DIM ** -0.5 * LOG2_E)
                + tables_t(DA_HEAD_DIM, None, 1.0))
        qt, k, vt, g = _proj_call(h2, mod3, p["pre_g"], p["w_in"].astype(BF16), tabs, one, one,
                                  kinds=["da_q"] * 2 + ["da_k"] * 2 + ["v"] * 2 + ["g"] * 2,
                                  tn=1024, o_cols=1024, v_rows=1024, tabs_transposed=True,
                                  rope_half=DA_HEAD_DIM // 4, **common)
        small = [p[n].reshape(1, DA_HEAD_DIM) for n in ("lam_q1", "lam_k1", "lam_q2", "lam_k2")]
        small.append(p["subln_g"].reshape(1, 2 * DA_HEAD_DIM))
        a3 = _attn_call("diff", qt, b3(k), vt, b3(g), small, seq=seq, need_ctx=need_ctx,
                        lambda_init=_lambda_init(layer))
    elif kind == "gqa":
        tabs = (tables_t(GQ_HEAD_DIM, p["q_norm_g"], GQ_HEAD_DIM ** -0.5 * LOG2_E)
                + tables_t(GQ_HEAD_DIM, p["k_norm_g"], 1.0))
        qt, k, vt, g = _proj_call(h2, mod3, p["pre_g"], p["w_in"].astype(BF16), tabs, one, one,
                                  kinds=["gq_q"] * 2 + ["gq_kv"] + ["g"] * 2,
                                  tn=1024, o_cols=GQ_KV_W, v_rows=GQ_KV_W, tabs_transposed=True,
                                  rope_half=GQ_HEAD_DIM // 4, **common)
        a3 = _attn_call("gqa", qt, b3(k), vt, b3(g), [], seq=seq, need_ctx=need_ctx)
    elif kind == "mla":
        w_in = p["w_in"]
        r2 = ML_Q_RANK + ML_KV_RANK
        tn = ML_Q_RANK
        w_pad = jnp.concatenate(
            [w_in[:, :r2 + ML_ROPE_DIM], jnp.zeros((d, tn - ML_ROPE_DIM), F32),
             w_in[:, r2 + ML_ROPE_DIM:]], axis=1).astype(BF16)
        qkv, g = _proj_call(h2, mod3, p["pre_g"], w_pad, _rope_tables(seq, ctx_len, ML_ROPE_DIM),
                            p["q_a_norm_g"].reshape(1, ML_Q_RANK),
                            p["kv_a_norm_g"].reshape(1, ML_KV_RANK),
                            kinds=["ml_qa", "ml_kva", "ml_kr"] + ["g"] * (d // tn),
                            tn=tn, o_cols=tn, v_rows=0, tabs_transposed=False,
                            rope_half=ML_ROPE_DIM // 4, **common)
        qd = ML_NOPE_DIM + ML_ROPE_DIM
        q_scale = qd ** -0.5 * LOG2_E
        wq = p["w_q_b"].reshape(ML_Q_RANK, ML_HEADS, qd)
        wq = jnp.pad(wq, ((0, 0), (0, 0), (0, 2 * LANES - qd)))
        wq = wq.reshape(ML_Q_RANK, ML_HEADS * 2 * LANES).astype(BF16)
        wkv = p["w_kv_b"].reshape(ML_KV_RANK, ML_HEADS, ML_NOPE_DIM + ML_V_DIM)
        wkv = jnp.concatenate([wkv[:, :, :ML_NOPE_DIM].reshape(ML_KV_RANK, -1),
                               wkv[:, :, ML_NOPE_DIM:].reshape(ML_KV_RANK, -1)],
                              axis=1).astype(BF16)
        qt, k, vt = _mla_b_call(qkv, wq, wkv, tables_t(ML_ROPE_DIM, None, q_scale), seq=seq,
                                ctx_len=ctx_len, q_scale=q_scale)
        a3 = _attn_call("mla", qt, b3(k), vt, b3(g), [], seq=seq, need_ctx=need_ctx)
    else:
        raise ValueError(kind)
    return _finish_call(a3, h3, mod3, p["post_g"], p["w_out"].astype(BF16), seq=seq)


def kernel(x, c, ctx, c_ctx, l0_ada_w, l0_ada_b, l0_pre_g, l0_post_g, l0_w_in, l0_lam_q1, l0_lam_k1, l0_lam_q2, l0_lam_k2, l0_subln_g, l0_w_out, l1_ada_w, l1_ada_b, l1_pre_g, l1_post_g, l1_w_in, l1_q_norm_g, l1_k_norm_g, l1_w_out, l2_ada_w, l2_ada_b, l2_pre_g, l2_post_g, l2_w_in, l2_q_a_norm_g, l2_w_q_b, l2_kv_a_norm_g, l2_w_kv_b, l2_w_out, l3_ada_w, l3_ada_b, l3_pre_g, l3_post_g, l3_w_in, l3_lam_q1, l3_lam_k1, l3_lam_q2, l3_lam_k2, l3_subln_g, l3_w_out):
    batch, seq, d = x.shape
    ctx_len = ctx.shape[1]
    assert d == D_MODEL and seq % GRID_W == 0 and batch + 1 <= MOD_ROWS
    diff_names = ("ada_w", "ada_b", "pre_g", "post_g", "w_in", "lam_q1", "lam_k1", "lam_q2",
                  "lam_k2", "subln_g", "w_out")
    layers = [
        ("diff", dict(zip(diff_names, (l0_ada_w, l0_ada_b, l0_pre_g, l0_post_g, l0_w_in,
                                       l0_lam_q1, l0_lam_k1, l0_lam_q2, l0_lam_k2, l0_subln_g,
                                       l0_w_out)))),
        ("gqa", dict(ada_w=l1_ada_w, ada_b=l1_ada_b, pre_g=l1_pre_g, post_g=l1_post_g,
                     w_in=l1_w_in, q_norm_g=l1_q_norm_g, k_norm_g=l1_k_norm_g, w_out=l1_w_out)),
        ("mla", dict(ada_w=l2_ada_w, ada_b=l2_ada_b, pre_g=l2_pre_g, post_g=l2_post_g,
                     w_in=l2_w_in, q_a_norm_g=l2_q_a_norm_g, w_q_b=l2_w_q_b,
                     kv_a_norm_g=l2_kv_a_norm_g, w_kv_b=l2_w_kv_b, w_out=l2_w_out)),
        ("diff", dict(zip(diff_names, (l3_ada_w, l3_ada_b, l3_pre_g, l3_post_g, l3_w_in,
                                       l3_lam_q1, l3_lam_k1, l3_lam_q2, l3_lam_k2, l3_subln_g,
                                       l3_w_out)))),
    ]
    cc = jnp.zeros((MOD_ROWS, d), F32).at[:batch].set(c).at[batch].set(c_ctx)
    h3 = jnp.concatenate([x, ctx], axis=1)
    for layer, (kind, p) in enumerate(layers):
        h3 = _layer(kind, layer, h3, cc, p, layer < len(layers) - 1, seq=seq, ctx_len=ctx_len)
    return h3
```

```python
import functools
import math

import jax
import jax.numpy as jnp
import numpy as np
from jax import lax
from jax.experimental import pallas as pl
from jax.experimental.pallas import tpu as pltpu

F32 = jnp.float32
BF16 = jnp.bfloat16
LOG2_E = math.log2(math.e)

D_MODEL = 2048
GRID_W = 64
ROPE_BASE = 10000.0
NORM_EPS = 1e-6
DA_HEAD_DIM = 64
DA_HEADS = D_MODEL // (2 * DA_HEAD_DIM)
GQ_HEAD_DIM = 128
GQ_HEADS = D_MODEL // GQ_HEAD_DIM
GQ_KV_HEADS = GQ_HEADS // 4
GQ_KV_W = GQ_KV_HEADS * GQ_HEAD_DIM
ML_NOPE_DIM = 128
ML_ROPE_DIM = 64
ML_V_DIM = 128
ML_HEADS = D_MODEL // ML_V_DIM
ML_Q_RANK = D_MODEL // 4
ML_KV_RANK = D_MODEL // 4

LANES = 128
SUBLANES = 8
MXU_N = 512
ROW_CHUNK = 128
VMEM_LIMIT = 56 * 1024 * 1024
MOD_ROWS = 16

PROJ_TM = 768
FINISH_TM = 384
FINISH_TM_LAST = 512
MLA_HEADS_PER_STEP = 2
DA_HEADS_PER_STEP = 2
KEY_CHUNK = 256


def _silu(x):
    return x / (1.0 + jnp.exp(-x))


def _rms(x, eps=NORM_EPS):
    return x * lax.rsqrt(jnp.mean(x * x, axis=-1, keepdims=True) + eps)


def _rope(x, cos, sin_a, sin_b, half):
    return (x * cos + pltpu.roll(x, LANES - half, 1) * sin_a + pltpu.roll(x, half, 1) * sin_b)


def _swap_halves(x, half):
    parts = []
    for r in range(0, x.shape[0], 2 * half):
        parts += [x[r + half:r + 2 * half], x[r:r + half]]
    return jnp.concatenate(parts, axis=0)


def _rope_t(x, cos_t, sin_t, half):
    return x * cos_t + _swap_halves(x, half) * sin_t


def _rope_tables(seq, ctx_len, head_dim):
    t = seq + ctx_len
    r = np.arange(t)
    lat = (r < seq)[:, None]
    rowpos = (r // GRID_W).astype(np.float32)[:, None]
    colpos = (r % GRID_W).astype(np.float32)[:, None]
    lane = np.arange(LANES)
    u = lane % head_dim
    half2 = head_dim // 2
    half = half2 // 2
    grp = (u // half2)[None, :]
    w = u % half2
    first = (w < half)[None, :]
    inv_freq = (np.float32(ROPE_BASE) ** (-(w % half).astype(np.float32) / np.float32(half)))[None, :]
    ang = (np.where(grp == 0, rowpos, colpos) * inv_freq).astype(np.float32)
    cos = np.where(lat, np.cos(ang), 1.0).astype(np.float32)
    sin = np.where(lat, np.sin(ang), 0.0).astype(np.float32)
    sin_a = np.where(first, -sin, np.float32(0.0))
    sin_b = np.where(first, np.float32(0.0), sin)
    return cos, sin_a, sin_b


def _rope_tables_t(seq, ctx_len, head_dim, gain, scale):
    cos, sin_a, sin_b = _rope_tables(seq, ctx_len, head_dim)
    half = head_dim // 4
    lane = np.arange(LANES)
    partner = np.where((lane % (2 * half)) < half, lane + half, lane - half)
    cos_t = jnp.asarray(np.ascontiguousarray((cos * np.float32(scale)).T))
    sin_t = jnp.asarray(np.ascontiguousarray(((sin_a + sin_b) * np.float32(scale)).T))
    if gain is None:
        return cos_t, sin_t
    g = jnp.tile(gain.astype(F32), LANES // head_dim)
    return cos_t * g[:, None], sin_t * g[partner][:, None]


def _ada_kernel(c_ref, w_ref, b_ref, o_ref):
    s = _silu(c_ref[...])
    o_ref[...] = jnp.dot(s.astype(BF16), w_ref[...].astype(BF16),
                         preferred_element_type=F32) + b_ref[...]


def _ada_call(cc, ada_w, ada_b):
    d, n = ada_w.shape
    tn = 768
    return pl.pallas_call(
        _ada_kernel,
        grid=(n // tn,),
        in_specs=[pl.BlockSpec((MOD_ROWS, d), lambda j: (0, 0)),
                  pl.BlockSpec((d, tn), lambda j: (0, j)),
                  pl.BlockSpec((1, tn), lambda j: (0, j))],
        out_specs=pl.BlockSpec((MOD_ROWS, tn), lambda j: (0, j)),
        out_shape=jax.ShapeDtypeStruct((MOD_ROWS, n), F32),
        compiler_params=pltpu.CompilerParams(dimension_semantics=("arbitrary",),
                                             vmem_limit_bytes=VMEM_LIMIT),
        name="ada",
    )(cc, ada_w, ada_b.reshape(1, n))


_KIND_OUTS = {"da_q": ("qt",), "gq_q": ("qt",), "da_k": ("o",), "gq_kv": ("o", "vt"),
              "v": ("vt",), "g": ("g",), "ml_qa": ("o",), "ml_kva": ("o",), "ml_kr": ("o",)}


def _mod_rows(mod_b_ref, mod_c_ref, is_ctx, part):
    d = D_MODEL
    mb = mod_b_ref[:, part * d:(part + 1) * d]
    mc = mod_c_ref[:, part * d:(part + 1) * d]
    return jnp.where(is_ctx, mc, mb)


def _prologue(h_ref, mod_b_ref, mod_c_ref, preg_ref, xn_ref, slot, row0, seq, pieces):
    preg = preg_ref[...]
    for c in pieces:
        rows = pl.ds(c * ROW_CHUNK, ROW_CHUNK)
        is_ctx = (row0 + c * ROW_CHUNK) >= seq
        shift = _mod_rows(mod_b_ref, mod_c_ref, is_ctx, 0)
        scale = _mod_rows(mod_b_ref, mod_c_ref, is_ctx, 1)
        y = _rms(h_ref[rows, :]) * preg
        xn_ref[slot, rows, :] = (y * (1.0 + scale) + shift).astype(BF16)


def _proj_kernel(*refs, kinds, out_names, n_tabs, seq, n_tiles, tiles_per_batch, rope_half):
    h_ref, mod_b_ref, mod_c_ref, preg_ref, w_ref = refs[:5]
    tabs = refs[5:5 + n_tabs]
    ng0_ref, ng1_ref = refs[5 + n_tabs:7 + n_tabs]
    outs = dict(zip(out_names, refs[7 + n_tabs:]))
    xn_ref = refs[-1]
    i = pl.program_id(0)
    j = pl.program_id(1)
    tm = h_ref.shape[0]
    n_steps = len(kinds)
    n_pieces = tm // ROW_CHUNK
    row0 = (jnp.minimum(i, n_tiles - 1) % tiles_per_batch) * tm
    prologue = functools.partial(_prologue, h_ref, mod_b_ref, mod_c_ref, preg_ref, xn_ref,
                                 i % 2, row0, seq)

    @pl.when((i == 0) & (j == 0))
    def _():
        prologue(range(n_pieces))

    def heads_t(a):
        at = a.T
        return [at[c * LANES:(c + 1) * LANES] for c in range(at.shape[0] // LANES)]

    def head_rms(xh):
        return lax.rsqrt(jnp.mean(xh * xh, axis=0, keepdims=True) + NORM_EPS)

    def tile(kind):
        a = jnp.dot(xn_ref[(i + 1) % 2], w_ref[...], preferred_element_type=F32)
        if kind == "g":
            outs["g"][...] = a
        elif kind == "v":
            outs["vt"][...] = a.T.astype(BF16)
        elif kind in ("da_q", "gq_q"):
            cos_t, sin_t = tabs[0][...], tabs[1][...]
            for c, xh in enumerate(heads_t(a)):
                y = _rope_t(xh, cos_t, sin_t, rope_half)
                if kind == "gq_q":
                    y = y * head_rms(xh)
                outs["qt"][c * LANES:(c + 1) * LANES, :] = y.astype(BF16)
        elif kind in ("da_k", "gq_kv"):
            cos_t, sin_t = tabs[2][...], tabs[3][...]
            k_cols = outs["o"].shape[1]
            for c, xh in enumerate(heads_t(a[:, :k_cols])):
                y = _rope_t(xh, cos_t, sin_t, rope_half)
                if kind == "gq_kv":
                    y = y * head_rms(xh)
                outs["o"][:, c * LANES:(c + 1) * LANES] = y.T.astype(BF16)
            if kind == "gq_kv":
                outs["vt"][...] = a[:, k_cols:].T.astype(BF16)
        elif kind == "ml_qa":
            outs["o"][...] = (_rms(a) * ng0_ref[...]).astype(BF16)
        elif kind == "ml_kva":
            outs["o"][...] = (_rms(a) * ng1_ref[...]).astype(BF16)
        elif kind == "ml_kr":
            outs["o"][...] = a.astype(BF16)
            outs["o"][:, :LANES] = _rope(a[:, :LANES], tabs[0][...], tabs[1][...], tabs[2][...],
                                         rope_half).astype(BF16)
        else:
            raise ValueError(kind)

    def step(jv):
        tile(kinds[jv])
        prologue([p for p in range(n_pieces) if n_steps - 1 - p % n_steps == jv])

    for jv in range(n_steps):
        pl.when((i > 0) & (j == jv))(functools.partial(step, jv))


def _proj_call(h2, mod3, pre_g, w, tabs, ng0, ng1, *, kinds, tn, o_cols, v_rows, tabs_transposed,
               batch, seq, ctx_len, rope_half):
    m, d = h2.shape
    n = w.shape[1]
    t = seq + ctx_len
    tm = PROJ_TM
    tpb = t // tm
    n_tiles = m // tm
    assert t % tm == 0 and n % tn == 0 and len(kinds) == n // tn

    def nxt(i):
        return jnp.minimum(i, n_tiles - 1)

    def cur(i):
        return jnp.maximum(i - 1, 0)

    def tile_idx(name):
        js = [jv for jv, k in enumerate(kinds) if name in _KIND_OUTS[k]]
        assert js == list(range(js[0], js[-1] + 1))
        return lambda i, j: jnp.where(i == 0, 0, jnp.clip(j - js[0], 0, len(js) - 1)), len(js)

    out_names, out_specs, out_shape = [], [], []
    for name in ("qt", "o", "vt", "g"):
        if not any(name in _KIND_OUTS[k] for k in kinds):
            continue
        idx, cnt = tile_idx(name)
        out_names.append(name)
        if name in ("qt", "vt"):
            rows = tn if name == "qt" else v_rows
            out_specs.append(pl.BlockSpec((rows, tm), lambda i, j, idx=idx: (idx(i, j), cur(i))))
            out_shape.append(jax.ShapeDtypeStruct((cnt * rows, m), BF16))
        else:
            cols, dt = (o_cols, BF16) if name == "o" else (tn, F32)
            out_specs.append(pl.BlockSpec((tm, cols), lambda i, j, idx=idx: (cur(i), idx(i, j))))
            out_shape.append(jax.ShapeDtypeStruct((m, cnt * cols), dt))
    assert out_shape[-1].shape == (m, d)

    if tabs_transposed:
        tab_spec = pl.BlockSpec((LANES, tm), lambda i, j: (0, cur(i) % tpb))
    else:
        tab_spec = pl.BlockSpec((tm, LANES), lambda i, j: (cur(i) % tpb, 0))
    kern = functools.partial(_proj_kernel, kinds=tuple(kinds), out_names=tuple(out_names),
                             n_tabs=len(tabs), seq=seq, n_tiles=n_tiles, tiles_per_batch=tpb,
                             rope_half=rope_half)
    return pl.pallas_call(
        kern,
        grid=(n_tiles + 1, n // tn),
        in_specs=[
            pl.BlockSpec((tm, d), lambda i, j: (nxt(i), 0)),
            pl.BlockSpec((None, 1, 3 * d), lambda i, j: (nxt(i) // tpb, 0, 0)),
            pl.BlockSpec((None, 1, 3 * d), lambda i, j: (batch, 0, 0)),
            pl.BlockSpec((1, d), lambda i, j: (0, 0)),
            pl.BlockSpec((d, tn), lambda i, j: (0, jnp.where(i == 0, 0, j))),
        ] + [tab_spec] * len(tabs) + [
            pl.BlockSpec(ng0.shape, lambda i, j: (0, 0)),
            pl.BlockSpec(ng1.shape, lambda i, j: (0, 0)),
        ],
        out_specs=out_specs,
        out_shape=out_shape,
        scratch_shapes=[pltpu.VMEM((2, tm, d), BF16)],
        compiler_params=pltpu.CompilerParams(dimension_semantics=("arbitrary", "arbitrary"),
                                             vmem_limit_bytes=VMEM_LIMIT),
        name="proj",
    )(h2, mod3, mod3, pre_g.reshape(1, d), w, *tabs, ng0, ng1)


def _mla_b_kernel(qa_ref, kva_ref, kr_ref, wq_ref, wkn_ref, wv_ref, cos_t_ref, sin_t_ref,
                  qt_ref, k_ref, vt_ref, *, q_scale):
    qt = jnp.dot(qa_ref[...], wq_ref[...], preferred_element_type=F32).T
    for c in range(qt.shape[0] // LANES):
        x = qt[c * LANES:(c + 1) * LANES]
        if c % 2 == 1:
            x = _rope_t(x, cos_t_ref[...], sin_t_ref[...], ML_ROPE_DIM // 4)
        else:
            x = x * q_scale
        qt_ref[c * LANES:(c + 1) * LANES, :] = x.astype(BF16)
    kva = kva_ref[...]
    kn = jnp.dot(kva, wkn_ref[...], preferred_element_type=F32)
    kr = kr_ref[...]
    for hh in range(kn.shape[1] // LANES):
        k_ref[:, 2 * hh * LANES:(2 * hh + 1) * LANES] = kn[:, hh * LANES:(hh + 1) * LANES].astype(BF16)
        k_ref[:, (2 * hh + 1) * LANES:(2 * hh + 2) * LANES] = kr
    vt_ref[...] = jnp.dot(kva, wv_ref[...], preferred_element_type=F32).T.astype(BF16)


def _mla_b_call(qkv, wq, wkv, tabs_t, *, seq, ctx_len, q_scale):
    m = qkv.shape[0]
    tm = PROJ_TM
    tpb = (seq + ctx_len) // tm
    r = ML_Q_RANK
    nq = wq.shape[1]
    nkv = wkv.shape[1] // 2
    steps = 4
    tq, tk = nq // steps, nkv // steps
    kr_col = (ML_Q_RANK + ML_KV_RANK) // LANES
    tab_spec = pl.BlockSpec((LANES, tm), lambda i, j: (0, i % tpb))
    return pl.pallas_call(
        functools.partial(_mla_b_kernel, q_scale=q_scale),
        grid=(m // tm, steps),
        in_specs=[
            pl.BlockSpec((tm, r), lambda i, j: (i, 0)),
            pl.BlockSpec((tm, r), lambda i, j: (i, 1)),
            pl.BlockSpec((tm, LANES), lambda i, j: (i, kr_col)),
            pl.BlockSpec((r, tq), lambda i, j: (0, j)),
            pl.BlockSpec((r, tk), lambda i, j: (0, j)),
            pl.BlockSpec((r, tk), lambda i, j: (0, steps + j)),
            tab_spec, tab_spec,
        ],
        out_specs=[
            pl.BlockSpec((tq, tm), lambda i, j: (j, i)),
            pl.BlockSpec((tm, 2 * tk), lambda i, j: (i, j)),
            pl.BlockSpec((tk, tm), lambda i, j: (j, i)),
        ],
        out_shape=[jax.ShapeDtypeStruct((nq, m), BF16),
                   jax.ShapeDtypeStruct((m, 2 * nkv), BF16),
                   jax.ShapeDtypeStruct((nkv, m), BF16)],
        compiler_params=pltpu.CompilerParams(dimension_semantics=("arbitrary", "arbitrary"),
                                             vmem_limit_bytes=VMEM_LIMIT),
        name="mla_b",
    )(qkv, qkv, qkv, wq, wkv, wkv, *tabs_t)


def _sublane_groups(x):
    return x.reshape(x.shape[0] // SUBLANES, SUBLANES, x.shape[1])


def _run_chains(chains, s_scr, e_scr):
    n = len(chains)
    shape, m, l = {}, {}, {}
    for step in range(n + 2):
        if step < n:
            qt, k, vt, _ = chains[step]
            qv = qt()
            nq, nk = qv.shape[1], vt().shape[1]
            shape[step] = (nk, nq)
            m8 = None
            for c0 in range(0, nk, KEY_CHUNK):
                st = jnp.dot(k(c0, KEY_CHUNK), qv, preferred_element_type=F32)
                s_scr[step % 2, c0:c0 + KEY_CHUNK, :nq] = st
                cm = jnp.max(_sublane_groups(st), axis=0)
                m8 = cm if m8 is None else jnp.maximum(m8, cm)
            m[step] = jnp.max(m8, axis=0, keepdims=True)
        i = step - 1
        if 0 <= i < n:
            nk, nq = shape[i]
            mi = m.pop(i)
            l8 = None
            for c0 in range(0, nk, KEY_CHUNK):
                p = jnp.exp2(s_scr[i % 2, c0:c0 + KEY_CHUNK, :nq] - mi)
                ps = jnp.sum(_sublane_groups(p), axis=0)
                l8 = ps if l8 is None else l8 + ps
                e_scr[i % 2, c0:c0 + KEY_CHUNK, :nq] = p.astype(BF16)
            l[i] = jnp.sum(l8, axis=0, keepdims=True)
        i = step - 2
        if 0 <= i < n:
            nk, nq = shape[i]
            _, _, vt, done = chains[i]
            done(jnp.dot(vt(), e_scr[i % 2, :nk, :nq], preferred_element_type=F32) / l.pop(i))


def _row_blocks(rows_q, seq):
    assert seq % MXU_N == 0
    blocks = [(r0, MXU_N, 0) for r0 in range(0, seq, MXU_N)]
    if rows_q > seq:
        blocks.append((seq, rows_q - seq, seq))
    return blocks


def _diff_attn_kernel(lq1_ref, lk1_ref, lq2_ref, lk2_ref, subg_ref, qt_ref, k_ref, vt_ref, g_ref,
                      o_ref, s_scr, e_scr, *, seq, lambda_init):
    lam = (jnp.exp(jnp.sum(lq1_ref[...] * lk1_ref[...], axis=-1, keepdims=True))
           - jnp.exp(jnp.sum(lq2_ref[...] * lk2_ref[...], axis=-1, keepdims=True))
           + lambda_init)
    chains = []
    for r0, nr, key0 in _row_blocks(o_ref.shape[0], seq):
        rows = slice(r0, r0 + nr)
        for hh in range(qt_ref.shape[0] // LANES):
            cols = slice(hh * LANES, (hh + 1) * LANES)
            parts = []

            def q_sub(sub, rows=rows, cols=cols):
                qf = qt_ref[cols, rows].astype(F32)
                row = lax.broadcasted_iota(jnp.int32, qf.shape, 0)
                return jnp.where((row >= DA_HEAD_DIM) == bool(sub), qf, 0.0).astype(BF16)

            def done(ot, rows=rows, cols=cols, parts=parts):
                parts.append(ot)
                if len(parts) == 2:
                    o = (parts[0] - lam * parts[1]).T
                    o = _rms(o) * subg_ref[...] * (1.0 - lambda_init)
                    o_ref[rows, cols] = (o * _silu(g_ref[rows, cols])).astype(BF16)

            for sub in range(2):
                chains.append((functools.partial(q_sub, sub),
                               lambda c0, kc, key0=key0, cols=cols: k_ref[key0 + c0:key0 + c0 + kc,
                                                                          cols],
                               lambda key0=key0, cols=cols: vt_ref[cols, key0:], done))
    _run_chains(chains, s_scr, e_scr)


def _plain_attn_kernel(qt_ref, k_ref, vt_ref, g_ref, o_ref, s_scr, e_scr, *, seq, dq, shared_kv):
    n_heads = qt_ref.shape[0] // dq
    chains = []
    for r0, nr, key0 in _row_blocks(o_ref.shape[0], seq):
        rows = slice(r0, r0 + nr)
        for hh in range(n_heads):
            kh = 0 if shared_kv else hh
            cols = slice(hh * LANES, (hh + 1) * LANES)

            def done(ot, rows=rows, cols=cols):
                o_ref[rows, cols] = (ot.T * _silu(g_ref[rows, cols])).astype(BF16)

            chains.append((lambda rows=rows, hh=hh: qt_ref[hh * dq:(hh + 1) * dq, rows],
                           lambda c0, kc, key0=key0, kh=kh: k_ref[key0 + c0:key0 + c0 + kc,
                                                                  kh * dq:(kh + 1) * dq],
                           lambda key0=key0, kh=kh: vt_ref[kh * LANES:(kh + 1) * LANES, key0:],
                           done))
    _run_chains(chains, s_scr, e_scr)


def _attn_call(kind, qt, k3, vt, g3, small, *, seq, need_ctx, lambda_init=None):
    batch, t, _ = k3.shape
    rows = t if need_ctx else seq
    heads = D_MODEL // LANES
    if kind == "diff":
        hps, kv_heads, dq, k_col0 = DA_HEADS_PER_STEP, DA_HEADS_PER_STEP, LANES, 0
        kern = functools.partial(_diff_attn_kernel, seq=seq, lambda_init=lambda_init)
    elif kind == "gqa":
        hps, kv_heads, dq, k_col0 = GQ_HEADS // GQ_KV_HEADS, 1, LANES, 0
        kern = functools.partial(_plain_attn_kernel, seq=seq, dq=dq, shared_kv=True)
    elif kind == "mla":
        hps, kv_heads, dq, k_col0 = MLA_HEADS_PER_STEP, MLA_HEADS_PER_STEP, 2 * LANES, 0
        kern = functools.partial(_plain_attn_kernel, seq=seq, dq=dq, shared_kv=False)
    else:
        raise ValueError(kind)
    kw = kv_heads * dq
    in_specs = [pl.BlockSpec(a.shape, lambda b, s: (0,) * a.ndim) for a in small] + [
        pl.BlockSpec((hps * dq, t), lambda b, s: (s, b)),
        pl.BlockSpec((None, t, kw), lambda b, s: (b, 0, k_col0 + s)),
        pl.BlockSpec((kv_heads * LANES, t), lambda b, s: (s, b)),
        pl.BlockSpec((None, rows, hps * LANES), lambda b, s: (b, 0, s)),
    ]
    return pl.pallas_call(
        kern,
        grid=(batch, heads // hps),
        in_specs=in_specs,
        out_specs=pl.BlockSpec((None, rows, hps * LANES), lambda b, s: (b, 0, s)),
        out_shape=jax.ShapeDtypeStruct((batch, rows, D_MODEL), BF16),
        scratch_shapes=[pltpu.VMEM((2, t, MXU_N), F32), pltpu.VMEM((2, t, MXU_N), BF16)],
        compiler_params=pltpu.CompilerParams(dimension_semantics=("arbitrary", "arbitrary"),
                                             vmem_limit_bytes=VMEM_LIMIT),
        name="attn_" + kind,
    )(*small, qt, k3, vt, g3)


def _finish_kernel(a_ref, h_ref, mod_b_ref, mod_c_ref, postg_ref, w_ref, o_ref, *, seq):
    ti = pl.program_id(1)
    tm = a_ref.shape[0]
    y = jnp.dot(a_ref[...], w_ref[...], preferred_element_type=F32)
    postg = postg_ref[...]
    for c in range(tm // ROW_CHUNK):
        rows = slice(c * ROW_CHUNK, (c + 1) * ROW_CHUNK)
        is_ctx = (ti * tm + c * ROW_CHUNK) >= seq
        gate = _mod_rows(mod_b_ref, mod_c_ref, is_ctx, 2)
        o_ref[rows, :] = h_ref[rows, :] + gate * (_rms(y[rows, :]) * postg)


def _finish_call(a3, h3, mod3, post_g, w, *, seq):
    batch, t, d = h3.shape
    rows_out = a3.shape[1]
    tm = FINISH_TM if rows_out == t else FINISH_TM_LAST
    assert rows_out % tm == 0
    blk = pl.BlockSpec((None, tm, d), lambda b, i: (b, i, 0))
    return pl.pallas_call(
        functools.partial(_finish_kernel, seq=seq),
        grid=(batch, rows_out // tm),
        in_specs=[
            blk, blk,
            pl.BlockSpec((None, 1, 3 * d), lambda b, i: (b, 0, 0)),
            pl.BlockSpec((None, 1, 3 * d), lambda b, i: (batch, 0, 0)),
            pl.BlockSpec((1, d), lambda b, i: (0, 0)),
            pl.BlockSpec((d, d), lambda b, i: (0, 0)),
        ],
        out_specs=blk,
        out_shape=jax.ShapeDtypeStruct((batch, rows_out, d), F32),
        compiler_params=pltpu.CompilerParams(dimension_semantics=("arbitrary", "arbitrary"),
                                             vmem_limit_bytes=VMEM_LIMIT),
        name="finish",
    )(a3, h3, mod3, mod3, post_g.reshape(1, d), w)


def _lambda_init(layer):
    return 0.8 - 0.6 * math.exp(-0.3 * layer)


def _layer(kind, layer, h3, cc, p, need_ctx, *, seq, ctx_len):
    batch, t, d = h3.shape
    m = batch * t
    mod3 = _ada_call(cc, p["ada_w"], p["ada_b"]).reshape(MOD_ROWS, 1, 3 * d)
    h2 = h3.reshape(m, d)
    one = jnp.ones((1, LANES), F32)
    common = dict(batch=batch, seq=seq, ctx_len=ctx_len)
    tables_t = functools.partial(_rope_tables_t, seq, ctx_len)

    def b3(a):
        return a.reshape(batch, t, a.shape[-1])

    if kind == "diff":
        tabs = (tables_t(DA_HEAD_DIM, None, DA_HEAD_DIM ** -0.5 * LOG2_E)
                + tables_t(DA_HEAD_DIM, None, 1.0))
        qt, k, vt, g = _proj_call(h2, mod3, p["pre_g"], p["w_in"].astype(BF16), tabs, one, one,
                                  kinds=["da_q"] * 2 + ["da_k"] * 2 + ["v"] * 2 + ["g"] * 2,
                                  tn=1024, o_cols=1024, v_rows=1024, tabs_transposed=True,
                                  rope_half=DA_HEAD_DIM // 4, **common)
        small = [p[n].reshape(1, DA_HEAD_DIM) for n in ("lam_q1", "lam_k1", "lam_q2", "lam_k2")]
        small.append(p["subln_g"].reshape(1, 2 * DA_HEAD_DIM))
        a3 = _attn_call("diff", qt, b3(k), vt, b3(g), small, seq=seq, need_ctx=need_ctx,
                        lambda_init=_lambda_init(layer))
    elif kind == "gqa":
        tabs = (tables_t(GQ_HEAD_DIM, p["q_norm_g"], GQ_HEAD_DIM ** -0.5 * LOG2_E)
                + tables_t(GQ_HEAD_DIM, p["k_norm_g"], 1.0))
        qt, k, vt, g = _proj_call(h2, mod3, p["pre_g"], p["w_in"].astype(BF16), tabs, one, one,
                                  kinds=["gq_q"] * 2 + ["gq_kv"] + ["g"] * 2,
                                  tn=1024, o_cols=GQ_KV_W, v_rows=GQ_KV_W, tabs_transposed=True,
                                  rope_half=GQ_HEAD_DIM // 4, **common)
        a3 = _attn_call("gqa", qt, b3(k), vt, b3(g), [], seq=seq, need_ctx=need_ctx)
    elif kind == "mla":
        w_in = p["w_in"]
        r2 = ML_Q_RANK + ML_KV_RANK
        tn = ML_Q_RANK
        w_pad = jnp.concatenate(
            [w_in[:, :r2 + ML_ROPE_DIM], jnp.zeros((d, tn - ML_ROPE_DIM), F32),
             w_in[:, r2 + ML_ROPE_DIM:]], axis=1).astype(BF16)
        qkv, g = _proj_call(h2, mod3, p["pre_g"], w_pad, _rope_tables(seq, ctx_len, ML_ROPE_DIM),
                            p["q_a_norm_g"].reshape(1, ML_Q_RANK),
                            p["kv_a_norm_g"].reshape(1, ML_KV_RANK),
                            kinds=["ml_qa", "ml_kva", "ml_kr"] + ["g"] * (d // tn),
                            tn=tn, o_cols=tn, v_rows=0, tabs_transposed=False,
                            rope_half=ML_ROPE_DIM // 4, **common)
        qd = ML_NOPE_DIM + ML_ROPE_DIM
        q_scale = qd ** -0.5 * LOG2_E
        wq = p["w_q_b"].reshape(ML_Q_RANK, ML_HEADS, qd)
        wq = jnp.pad(wq, ((0, 0), (0, 0), (0, 2 * LANES - qd)))
        wq = wq.reshape(ML_Q_RANK, ML_HEADS * 2 * LANES).astype(BF16)
        wkv = p["w_kv_b"].reshape(ML_KV_RANK, ML_HEADS, ML_NOPE_DIM + ML_V_DIM)
        wkv = jnp.concatenate([wkv[:, :, :ML_NOPE_DIM].reshape(ML_KV_RANK, -1),
                               wkv[:, :, ML_NOPE_DIM:].reshape(ML_KV_RANK, -1)],
                              axis=1).astype(BF16)
        qt, k, vt = _mla_b_call(qkv, wq, wkv, tables_t(ML_ROPE_DIM, None, q_scale), seq=seq,
                                ctx_len=ctx_len, q_scale=q_scale)
        a3 = _attn_call("mla", qt, b3(k), vt, b3(g), [], seq=seq, need_ctx=need_ctx)
    else:
        raise ValueError(kind)
    return _finish_call(a3, h3, mod3, p["post_g"], p["w_out"].astype(BF16), seq=seq)


def kernel(x, c, ctx, c_ctx, l0_ada_w, l0_ada_b, l0_pre_g, l0_post_g, l0_w_in, l0_lam_q1, l0_lam_k1, l0_lam_q2, l0_lam_k2, l0_subln_g, l0_w_out, l1_ada_w, l1_ada_b, l1_pre_g, l1_post_g, l1_w_in, l1_q_norm_g, l1_k_norm_g, l1_w_out, l2_ada_w, l2_ada_b, l2_pre_g, l2_post_g, l2_w_in, l2_q_a_norm_g, l2_w_q_b, l2_kv_a_norm_g, l2_w_kv_b, l2_w_out, l3_ada_w, l3_ada_b, l3_pre_g, l3_post_g, l3_w_in, l3_lam_q1, l3_lam_k1, l3_lam_q2, l3_lam_k2, l3_subln_g, l3_w_out):
    batch, seq, d = x.shape
    ctx_len = ctx.shape[1]
    assert d == D_MODEL and seq % GRID_W == 0 and batch + 1 <= MOD_ROWS
    diff_names = ("ada_w", "ada_b", "pre_g", "post_g", "w_in", "lam_q1", "lam_k1", "lam_q2",
                  "lam_k2", "subln_g", "w_out")
    layers = [
        ("diff", dict(zip(diff_names, (l0_ada_w, l0_ada_b, l0_pre_g, l0_post_g, l0_w_in,
                                       l0_lam_q1, l0_lam_k1, l0_lam_q2, l0_lam_k2, l0_subln_g,
                                       l0_w_out)))),
        ("gqa", dict(ada_w=l1_ada_w, ada_b=l1_ada_b, pre_g=l1_pre_g, post_g=l1_post_g,
                     w_in=l1_w_in, q_norm_g=l1_q_norm_g, k_norm_g=l1_k_norm_g, w_out=l1_w_out)),
        ("mla", dict(ada_w=l2_ada_w, ada_b=l2_ada_b, pre_g=l2_pre_g, post_g=l2_post_g,
                     w_in=l2_w_in, q_a_norm_g=l2_q_a_norm_g, w_q_b=l2_w_q_b,
                     kv_a_norm_g=l2_kv_a_norm_g, w_kv_b=l2_w_kv_b, w_out=l2_w_out)),
        ("diff", dict(zip(diff_names, (l3_ada_w, l3_ada_b, l3_pre_g, l3_post_g, l3_w_in,
                                       l3_lam_q1, l3_lam_k1, l3_lam_q2, l3_lam_k2, l3_subln_g,
                                       l3_w_out)))),
    ]
    cc = jnp.zeros((MOD_ROWS, d), F32).at[:batch].set(c).at[batch].set(c_ctx)
    h3 = jnp.concatenate([x, ctx], axis=1)
    for layer, (kind, p) in enumerate(layers):
        h3 = _layer(kind, layer, h3, cc, p, layer < len(layers) - 1, seq=seq, ctx_len=ctx_len)
    return h3
```

```python
import functools
import math

import jax
import jax.numpy as jnp
import numpy as np
from jax import lax
from jax.experimental import pallas as pl
from jax.experimental.pallas import tpu as pltpu

F32 = jnp.float32
BF16 = jnp.bfloat16
LOG2_E = math.log2(math.e)

D_MODEL = 2048
GRID_W = 64
ROPE_BASE = 10000.0
NORM_EPS = 1e-6
DA_HEAD_DIM = 64
DA_HEADS = D_MODEL // (2 * DA_HEAD_DIM)
GQ_HEAD_DIM = 128
GQ_HEADS = D_MODEL // GQ_HEAD_DIM
GQ_KV_HEADS = GQ_HEADS // 4
GQ_KV_W = GQ_KV_HEADS * GQ_HEAD_DIM
ML_NOPE_DIM = 128
ML_ROPE_DIM = 64
ML_V_DIM = 128
ML_HEADS = D_MODEL // ML_V_DIM
ML_Q_RANK = D_MODEL // 4
ML_KV_RANK = D_MODEL // 4

LANES = 128
SUBLANES = 8
MXU_N = 512
ROW_CHUNK = 128
VMEM_LIMIT = 56 * 1024 * 1024
MOD_ROWS = 16

PROJ_TM = 768
FINISH_TM = 384
FINISH_TM_LAST = 512
MLA_HEADS_PER_STEP = 2
DA_HEADS_PER_STEP = 2
KEY_CHUNK = 256


def _silu(x):
    return x / (1.0 + jnp.exp(-x))


def _rms(x, eps=NORM_EPS):
    return x * lax.rsqrt(jnp.mean(x * x, axis=-1, keepdims=True) + eps)


def _rope(x, cos, sin_a, sin_b, half):
    return (x * cos + pltpu.roll(x, LANES - half, 1) * sin_a + pltpu.roll(x, half, 1) * sin_b)


def _swap_halves(x, half):
    parts = []
    for r in range(0, x.shape[0], 2 * half):
        parts += [x[r + half:r + 2 * half], x[r:r + half]]
    return jnp.concatenate(parts, axis=0)


def _rope_t(x, cos_t, sin_t, half):
    return x * cos_t + _swap_halves(x, half) * sin_t


def _rope_tables(seq, ctx_len, head_dim):
    t = seq + ctx_len
    r = np.arange(t)
    lat = (r < seq)[:, None]
    rowpos = (r // GRID_W).astype(np.float32)[:, None]
    colpos = (r % GRID_W).astype(np.float32)[:, None]
    lane = np.arange(LANES)
    u = lane % head_dim
    half2 = head_dim // 2
    half = half2 // 2
    grp = (u // half2)[None, :]
    w = u % half2
    first = (w < half)[None, :]
    inv_freq = (np.float32(ROPE_BASE) ** (-(w % half).astype(np.float32) / np.float32(half)))[None, :]
    ang = (np.where(grp == 0, rowpos, colpos) * inv_freq).astype(np.float32)
    cos = np.where(lat, np.cos(ang), 1.0).astype(np.float32)
    sin = np.where(lat, np.sin(ang), 0.0).astype(np.float32)
    sin_a = np.where(first, -sin, np.float32(0.0))
    sin_b = np.where(first, np.float32(0.0), sin)
    return cos, sin_a, sin_b


def _rope_tables_t(seq, ctx_len, head_dim, gain, scale):
    cos, sin_a, sin_b = _rope_tables(seq, ctx_len, head_dim)
    half = head_dim // 4
    lane = np.arange(LANES)
    partner = np.where((lane % (2 * half)) < half, lane + half, lane - half)
    cos_t = jnp.asarray(np.ascontiguousarray((cos * np.float32(scale)).T))
    sin_t = jnp.asarray(np.ascontiguousarray(((sin_a + sin_b) * np.float32(scale)).T))
    if gain is None:
        return cos_t, sin_t
    g = jnp.tile(gain.astype(F32), LANES // head_dim)
    return cos_t * g[:, None], sin_t * g[partner][:, None]


def _ada_kernel(c_ref, w_ref, b_ref, o_ref):
    s = _silu(c_ref[...])
    o_ref[...] = jnp.dot(s.astype(BF16), w_ref[...].astype(BF16),
                         preferred_element_type=F32) + b_ref[...]


def _ada_call(cc, ada_w, ada_b):
    d, n = ada_w.shape
    tn = 768
    return pl.pallas_call(
        _ada_kernel,
        grid=(n // tn,),
        in_specs=[pl.BlockSpec((MOD_ROWS, d), lambda j: (0, 0)),
                  pl.BlockSpec((d, tn), lambda j: (0, j)),
                  pl.BlockSpec((1, tn), lambda j: (0, j))],
        out_specs=pl.BlockSpec((MOD_ROWS, tn), lambda j: (0, j)),
        out_shape=jax.ShapeDtypeStruct((MOD_ROWS, n), F32),
        compiler_params=pltpu.CompilerParams(dimension_semantics=("arbitrary",),
                                             vmem_limit_bytes=VMEM_LIMIT),
        name="ada",
    )(cc, ada_w, ada_b.reshape(1, n))


_KIND_OUTS = {"da_q": ("qt",), "gq_q": ("qt",), "da_k": ("o",), "gq_kv": ("o", "vt"),
              "v": ("vt",), "g": ("g",), "ml_qa": ("o",), "ml_kva": ("o",), "ml_kr": ("o",)}


def _mod_rows(mod_b_ref, mod_c_ref, is_ctx, part):
    d = D_MODEL
    mb = mod_b_ref[:, part * d:(part + 1) * d]
    mc = mod_c_ref[:, part * d:(part + 1) * d]
    return jnp.where(is_ctx, mc, mb)


def _prologue(h_ref, mod_b_ref, mod_c_ref, preg_ref, xn_ref, slot, row0, seq, pieces):
    preg = preg_ref[...]
    for c in pieces:
        rows = pl.ds(c * ROW_CHUNK, ROW_CHUNK)
        is_ctx = (row0 + c * ROW_CHUNK) >= seq
        shift = _mod_rows(mod_b_ref, mod_c_ref, is_ctx, 0)
        scale = _mod_rows(mod_b_ref, mod_c_ref, is_ctx, 1)
        y = _rms(h_ref[rows, :]) * preg
        xn_ref[slot, rows, :] = (y * (1.0 + scale) + shift).astype(BF16)


def _proj_kernel(*refs, kinds, out_names, n_tabs, seq, n_tiles, tiles_per_batch, rope_half):
    h_ref, mod_b_ref, mod_c_ref, preg_ref, w_ref = refs[:5]
    tabs = refs[5:5 + n_tabs]
    ng0_ref, ng1_ref = refs[5 + n_tabs:7 + n_tabs]
    outs = dict(zip(out_names, refs[7 + n_tabs:]))
    xn_ref = refs[-1]
    i = pl.program_id(0)
    j = pl.program_id(1)
    tm = h_ref.shape[0]
    n_steps = len(kinds)
    n_pieces = tm // ROW_CHUNK
    row0 = (jnp.minimum(i, n_tiles - 1) % tiles_per_batch) * tm
    prologue = functools.partial(_prologue, h_ref, mod_b_ref, mod_c_ref, preg_ref, xn_ref,
                                 i % 2, row0, seq)

    @pl.when((i == 0) & (j == 0))
    def _():
        prologue(range(n_pieces))

    def heads_t(a):
        at = a.T
        return [at[c * LANES:(c + 1) * LANES] for c in range(at.shape[0] // LANES)]

    def head_rms(xh):
        return lax.rsqrt(jnp.mean(xh * xh, axis=0, keepdims=True) + NORM_EPS)

    def tile(kind):
        a = jnp.dot(xn_ref[(i + 1) % 2], w_ref[...], preferred_element_type=F32)
        if kind == "g":
            outs["g"][...] = a
        elif kind == "v":
            outs["vt"][...] = a.T.astype(BF16)
        elif kind in ("da_q", "gq_q"):
            cos_t, sin_t = tabs[0][...], tabs[1][...]
            for c, xh in enumerate(heads_t(a)):
                y = _rope_t(xh, cos_t, sin_t, rope_half)
                if kind == "gq_q":
                    y = y * head_rms(xh)
                outs["qt"][c * LANES:(c + 1) * LANES, :] = y.astype(BF16)
        elif kind in ("da_k", "gq_kv"):
            cos_t, sin_t = tabs[2][...], tabs[3][...]
            k_cols = outs["o"].shape[1]
            for c, xh in enumerate(heads_t(a[:, :k_cols])):
                y = _rope_t(xh, cos_t, sin_t, rope_half)
                if kind == "gq_kv":
                    y = y * head_rms(xh)
                outs["o"][:, c * LANES:(c + 1) * LANES] = y.T.astype(BF16)
            if kind == "gq_kv":
                outs["vt"][...] = a[:, k_cols:].T.astype(BF16)
        elif kind == "ml_qa":
            outs["o"][...] = (_rms(a) * ng0_ref[...]).astype(BF16)
        elif kind == "ml_kva":
            outs["o"][...] = (_rms(a) * ng1_ref[...]).astype(BF16)
        elif kind == "ml_kr":
            outs["o"][...] = a.astype(BF16)
            outs["o"][:, :LANES] = _rope(a[:, :LANES], tabs[0][...], tabs[1][...], tabs[2][...],
                                         rope_half).astype(BF16)
        else:
            raise ValueError(kind)

    def step(jv):
        tile(kinds[jv])
        prologue([p for p in range(n_pieces) if n_steps - 1 - p % n_steps == jv])

    for jv in range(n_steps):
        pl.when((i > 0) & (j == jv))(functools.partial(step, jv))


def _proj_call(h2, mod3, pre_g, w, tabs, ng0, ng1, *, kinds, tn, o_cols, v_rows, tabs_transposed,
               batch, seq, ctx_len, rope_half):
    m, d = h2.shape
    n = w.shape[1]
    t = seq + ctx_len
    tm = PROJ_TM
    tpb = t // tm
    n_tiles = m // tm
    assert t % tm == 0 and n % tn == 0 and len(kinds) == n // tn

    def nxt(i):
        return jnp.minimum(i, n_tiles - 1)

    def cur(i):
        return jnp.maximum(i - 1, 0)

    def tile_idx(name):
        js = [jv for jv, k in enumerate(kinds) if name in _KIND_OUTS[k]]
        assert js == list(range(js[0], js[-1] + 1))
        return lambda i, j: jnp.where(i == 0, 0, jnp.clip(j - js[0], 0, len(js) - 1)), len(js)

    out_names, out_specs, out_shape = [], [], []
    for name in ("qt", "o", "vt", "g"):
        if not any(name in _KIND_OUTS[k] for k in kinds):
            continue
        idx, cnt = tile_idx(name)
        out_names.append(name)
        if name in ("qt", "vt"):
            rows = tn if name == "qt" else v_rows
            out_specs.append(pl.BlockSpec((rows, tm), lambda i, j, idx=idx: (idx(i, j), cur(i))))
            out_shape.append(jax.ShapeDtypeStruct((cnt * rows, m), BF16))
        else:
            cols, dt = (o_cols, BF16) if name == "o" else (tn, F32)
            out_specs.append(pl.BlockSpec((tm, cols), lambda i, j, idx=idx: (cur(i), idx(i, j))))
            out_shape.append(jax.ShapeDtypeStruct((m, cnt * cols), dt))
    assert out_shape[-1].shape == (m, d)

    if tabs_transposed:
        tab_spec = pl.BlockSpec((LANES, tm), lambda i, j: (0, cur(i) % tpb))
    else:
        tab_spec = pl.BlockSpec((tm, LANES), lambda i, j: (cur(i) % tpb, 0))
    kern = functools.partial(_proj_kernel, kinds=tuple(kinds), out_names=tuple(out_names),
                             n_tabs=len(tabs), seq=seq, n_tiles=n_tiles, tiles_per_batch=tpb,
                             rope_half=rope_half)
    return pl.pallas_call(
        kern,
        grid=(n_tiles + 1, n // tn),
        in_specs=[
            pl.BlockSpec((tm, d), lambda i, j: (nxt(i), 0)),
            pl.BlockSpec((None, 1, 3 * d), lambda i, j: (nxt(i) // tpb, 0, 0)),
            pl.BlockSpec((None, 1, 3 * d), lambda i, j: (batch, 0, 0)),
            pl.BlockSpec((1, d), lambda i, j: (0, 0)),
            pl.BlockSpec((d, tn), lambda i, j: (0, jnp.where(i == 0, 0, j))),
        ] + [tab_spec] * len(tabs) + [
            pl.BlockSpec(ng0.shape, lambda i, j: (0, 0)),
            pl.BlockSpec(ng1.shape, lambda i, j: (0, 0)),
        ],
        out_specs=out_specs,
        out_shape=out_shape,
        scratch_shapes=[pltpu.VMEM((2, tm, d), BF16)],
        compiler_params=pltpu.CompilerParams(dimension_semantics=("arbitrary", "arbitrary"),
                                             vmem_limit_bytes=VMEM_LIMIT),
        name="proj",
    )(h2, mod3, mod3, pre_g.reshape(1, d), w, *tabs, ng0, ng1)


def _mla_b_kernel(qa_ref, kva_ref, kr_ref, wq_ref, wkn_ref, wv_ref, cos_t_ref, sin_t_ref,
                  qt_ref, k_ref, vt_ref, *, q_scale):
    qt = jnp.dot(qa_ref[...], wq_ref[...], preferred_element_type=F32).T
    for c in range(qt.shape[0] // LANES):
        x = qt[c * LANES:(c + 1) * LANES]
        if c % 2 == 1:
            x = _rope_t(x, cos_t_ref[...], sin_t_ref[...], ML_ROPE_DIM // 4)
        else:
            x = x * q_scale
        qt_ref[c * LANES:(c + 1) * LANES, :] = x.astype(BF16)
    kva = kva_ref[...]
    kn = jnp.dot(kva, wkn_ref[...], preferred_element_type=F32)
    kr = kr_ref[...]
    for hh in range(kn.shape[1] // LANES):
        k_ref[:, 2 * hh * LANES:(2 * hh + 1) * LANES] = kn[:, hh * LANES:(hh + 1) * LANES].astype(BF16)
        k_ref[:, (2 * hh + 1) * LANES:(2 * hh + 2) * LANES] = kr
    vt_ref[...] = jnp.dot(kva, wv_ref[...], preferred_element_type=F32).T.astype(BF16)


def _mla_b_call(qkv, wq, wkv, tabs_t, *, seq, ctx_len, q_scale):
    m = qkv.shape[0]
    tm = PROJ_TM
    tpb = (seq + ctx_len) // tm
    r = ML_Q_RANK
    nq = wq.shape[1]
    nkv = wkv.shape[1] // 2
    steps = 4
    tq, tk = nq // steps, nkv // steps
    kr_col = (ML_Q_RANK + ML_KV_RANK) // LANES
    tab_spec = pl.BlockSpec((LANES, tm), lambda i, j: (0, i % tpb))
    return pl.pallas_call(
        functools.partial(_mla_b_kernel, q_scale=q_scale),
        grid=(m // tm, steps),
        in_specs=[
            pl.BlockSpec((tm, r), lambda i, j: (i, 0)),
            pl.BlockSpec((tm, r), lambda i, j: (i, 1)),
            pl.BlockSpec((tm, LANES), lambda i, j: (i, kr_col)),
            pl.BlockSpec((r, tq), lambda i, j: (0, j)),
            pl.BlockSpec((r, tk), lambda i, j: (0, j)),
            pl.BlockSpec((r, tk), lambda i, j: (0, steps + j)),
            tab_spec, tab_spec,
        ],
        out_specs=[
            pl.BlockSpec((tq, tm), lambda i, j: (j, i)),
            pl.BlockSpec((tm, 2 * tk), lambda i, j: (i, j)),
            pl.BlockSpec((tk, tm), lambda i, j: (j, i)),
        ],
        out_shape=[jax.ShapeDtypeStruct((nq, m), BF16),
                   jax.ShapeDtypeStruct((m, 2 * nkv), BF16),
                   jax.ShapeDtypeStruct((nkv, m), BF16)],
        compiler_params=pltpu.CompilerParams(dimension_semantics=("arbitrary", "arbitrary"),
                                             vmem_limit_bytes=VMEM_LIMIT),
        name="mla_b",
    )(qkv, qkv, qkv, wq, wkv, wkv, *tabs_t)


def _sublane_groups(x):
    return x.reshape(x.shape[0] // SUBLANES, SUBLANES, x.shape[1])


def _run_chains(chains, s_scr, e_scr):
    n = len(chains)
    shape, m, l = {}, {}, {}
    for step in range(n + 2):
        if step < n:
            qt, k, vt, _ = chains[step]
            qv = qt()
            nq, nk = qv.shape[1], vt().shape[1]
            shape[step] = (nk, nq)
            kc = min(KEY_CHUNK, nk)
            m8 = None
            for c0 in range(0, nk, kc):
                st = jnp.dot(k(c0, kc), qv, preferred_element_type=F32)
                s_scr[step % 2, c0:c0 + kc, :nq] = st
                cm = jnp.max(_sublane_groups(st), axis=0)
                m8 = cm if m8 is None else jnp.maximum(m8, cm)
            m[step] = jnp.max(m8, axis=0, keepdims=True)
        i = step - 1
        if 0 <= i < n:
            nk, nq = shape[i]
            mi = m.pop(i)
            kc = min(KEY_CHUNK, nk)
            l8 = None
            for c0 in range(0, nk, kc):
                p = jnp.exp2(s_scr[i % 2, c0:c0 + kc, :nq] - mi)
                ps = jnp.sum(_sublane_groups(p), axis=0)
                l8 = ps if l8 is None else l8 + ps
                e_scr[i % 2, c0:c0 + kc, :nq] = p.astype(BF16)
            l[i] = jnp.sum(l8, axis=0, keepdims=True)
        i = step - 2
        if 0 <= i < n:
            nk, nq = shape[i]
            _, _, vt, done = chains[i]
            done(jnp.dot(vt(), e_scr[i % 2, :nk, :nq], preferred_element_type=F32) / l.pop(i))


def _row_blocks(rows_q, seq):
    assert seq % MXU_N == 0
    blocks = [(r0, MXU_N, 0) for r0 in range(0, seq, MXU_N)]
    if rows_q > seq:
        blocks.append((seq, rows_q - seq, seq))
    return blocks


def _diff_attn_kernel(lq1_ref, lk1_ref, lq2_ref, lk2_ref, subg_ref, qt_ref, k_ref, vt_ref, g_ref,
                      o_ref, s_scr, e_scr, *, seq, lambda_init):
    lam = (jnp.exp(jnp.sum(lq1_ref[...] * lk1_ref[...], axis=-1, keepdims=True))
           - jnp.exp(jnp.sum(lq2_ref[...] * lk2_ref[...], axis=-1, keepdims=True))
           + lambda_init)
    chains = []
    for r0, nr, key0 in _row_blocks(o_ref.shape[0], seq):
        rows = slice(r0, r0 + nr)
        for hh in range(qt_ref.shape[0] // LANES):
            cols = slice(hh * LANES, (hh + 1) * LANES)
            parts = []

            def q_sub(sub, rows=rows, cols=cols):
                qf = qt_ref[cols, rows].astype(F32)
                row = lax.broadcasted_iota(jnp.int32, qf.shape, 0)
                return jnp.where((row >= DA_HEAD_DIM) == bool(sub), qf, 0.0).astype(BF16)

            def done(ot, rows=rows, cols=cols, parts=parts):
                parts.append(ot)
                if len(parts) == 2:
                    o = (parts[0] - lam * parts[1]).T
                    o = _rms(o) * subg_ref[...] * (1.0 - lambda_init)
                    o_ref[rows, cols] = (o * _silu(g_ref[rows, cols])).astype(BF16)

            for sub in range(2):
                chains.append((functools.partial(q_sub, sub),
                               lambda c0, kc, key0=key0, cols=cols: k_ref[key0 + c0:key0 + c0 + kc,
                                                                          cols],
                               lambda key0=key0, cols=cols: vt_ref[cols, key0:], done))
    _run_chains(chains, s_scr, e_scr)


def _plain_attn_kernel(qt_ref, k_ref, vt_ref, g_ref, o_ref, s_scr, e_scr, *, seq, dq, shared_kv):
    n_heads = qt_ref.shape[0] // dq
    chains = []
    for r0, nr, key0 in _row_blocks(o_ref.shape[0], seq):
        rows = slice(r0, r0 + nr)
        for hh in range(n_heads):
            kh = 0 if shared_kv else hh
            cols = slice(hh * LANES, (hh + 1) * LANES)

            def done(ot, rows=rows, cols=cols):
                o_ref[rows, cols] = (ot.T * _silu(g_ref[rows, cols])).astype(BF16)

            chains.append((lambda rows=rows, hh=hh: qt_ref[hh * dq:(hh + 1) * dq, rows],
                           lambda c0, kc, key0=key0, kh=kh: k_ref[key0 + c0:key0 + c0 + kc,
                                                                  kh * dq:(kh + 1) * dq],
                           lambda key0=key0, kh=kh: vt_ref[kh * LANES:(kh + 1) * LANES, key0:],
                           done))
    _run_chains(chains, s_scr, e_scr)


def _attn_call(kind, qt, k3, vt, g3, small, *, seq, need_ctx, lambda_init=None):
    batch, t, _ = k3.shape
    rows = t if need_ctx else seq
    heads = D_MODEL // LANES
    if kind == "diff":
        hps, kv_heads, dq, k_col0 = DA_HEADS_PER_STEP, DA_HEADS_PER_STEP, LANES, 0
        kern = functools.partial(_diff_attn_kernel, seq=seq, lambda_init=lambda_init)
    elif kind == "gqa":
        hps, kv_heads, dq, k_col0 = GQ_HEADS // GQ_KV_HEADS, 1, LANES, 0
        kern = functools.partial(_plain_attn_kernel, seq=seq, dq=dq, shared_kv=True)
    elif kind == "mla":
        hps, kv_heads, dq, k_col0 = MLA_HEADS_PER_STEP, MLA_HEADS_PER_STEP, 2 * LANES, 0
        kern = functools.partial(_plain_attn_kernel, seq=seq, dq=dq, shared_kv=False)
    else:
        raise ValueError(kind)
    kw = kv_heads * dq
    in_specs = [pl.BlockSpec(a.shape, lambda b, s: (0,) * a.ndim) for a in small] + [
        pl.BlockSpec((hps * dq, t), lambda b, s: (s, b)),
        pl.BlockSpec((None, t, kw), lambda b, s: (b, 0, k_col0 + s)),
        pl.BlockSpec((kv_heads * LANES, t), lambda b, s: (s, b)),
        pl.BlockSpec((None, rows, hps * LANES), lambda b, s: (b, 0, s)),
    ]
    return pl.pallas_call(
        kern,
        grid=(batch, heads // hps),
        in_specs=in_specs,
        out_specs=pl.BlockSpec((None, rows, hps * LANES), lambda b, s: (b, 0, s)),
        out_shape=jax.ShapeDtypeStruct((batch, rows, D_MODEL), BF16),
        scratch_shapes=[pltpu.VMEM((2, t, MXU_N), F32), pltpu.VMEM((2, t, MXU_N), BF16)],
        compiler_params=pltpu.CompilerParams(dimension_semantics=("arbitrary", "arbitrary"),
                                             vmem_limit_bytes=VMEM_LIMIT),
        name="attn_" + kind,
    )(*small, qt, k3, vt, g3)


def _finish_kernel(*refs, seq, n_cast):
    a_ref, h_ref, mod_b_ref, mod_c_ref, postg_ref, w_ref = refs[:6]
    n_in = 6 + (3 + n_cast if n_cast else 0)
    o_ref = refs[n_in]
    ti = pl.program_id(1)
    tm = a_ref.shape[0]
    y = jnp.dot(a_ref[...], w_ref[...], preferred_element_type=F32)
    postg = postg_ref[...]
    for c in range(tm // ROW_CHUNK):
        rows = slice(c * ROW_CHUNK, (c + 1) * ROW_CHUNK)
        is_ctx = (ti * tm + c * ROW_CHUNK) >= seq
        gate = _mod_rows(mod_b_ref, mod_c_ref, is_ctx, 2)
        o_ref[rows, :] = h_ref[rows, :] + gate * (_rms(y[rows, :]) * postg)
    if n_cast:
        c_ref, adaw_ref, adab_ref = refs[6:9]
        _ada_kernel(c_ref, adaw_ref, adab_ref, refs[n_in + 1])
        for src, dst in zip(refs[9:n_in], refs[n_in + 2:]):
            dst[...] = src[...].astype(BF16)


def _finish_call(a3, h3, mod3, post_g, w, *, seq, nxt=None):
    batch, t, d = h3.shape
    rows_out = a3.shape[1]
    tm = FINISH_TM if rows_out == t else FINISH_TM_LAST
    assert rows_out % tm == 0
    n_i = rows_out // tm
    blk = pl.BlockSpec((None, tm, d), lambda b, i: (b, i, 0))
    in_specs = [
        blk, blk,
        pl.BlockSpec((None, 1, 3 * d), lambda b, i: (b, 0, 0)),
        pl.BlockSpec((None, 1, 3 * d), lambda b, i: (batch, 0, 0)),
        pl.BlockSpec((1, d), lambda b, i: (0, 0)),
        pl.BlockSpec((d, d), lambda b, i: (0, 0)),
    ]
    args = [a3, h3, mod3, mod3, post_g.reshape(1, d), w]
    out_specs = [blk]
    out_shape = [jax.ShapeDtypeStruct((batch, rows_out, d), F32)]
    names = []
    if nxt is not None:
        cc, ada_w, ada_b, weights = nxt
        n_mod = ada_w.shape[1]
        assert n_mod == batch * n_i * LANES
        flat = lambda b, i: (0, b * n_i + i)
        in_specs += [pl.BlockSpec((MOD_ROWS, d), lambda b, i: (0, 0)),
                     pl.BlockSpec((d, LANES), flat), pl.BlockSpec((1, LANES), flat)]
        args += [cc, ada_w, ada_b.reshape(1, n_mod)]
        out_specs.append(pl.BlockSpec((MOD_ROWS, LANES), flat))
        out_shape.append(jax.ShapeDtypeStruct((MOD_ROWS, n_mod), F32))
        for name, wf in weights.items():
            r, c = wf.shape
            assert r % (batch * 2 * SUBLANES) == 0 and c % (4 * LANES) == 0 and n_i >= 4
            spec = pl.BlockSpec((r // batch, c // 4), lambda b, i: (b, jnp.minimum(i, 3)))
            names.append(name)
            in_specs.append(spec)
            args.append(wf)
            out_specs.append(spec)
            out_shape.append(jax.ShapeDtypeStruct((r, c), BF16))
    outs = pl.pallas_call(
        functools.partial(_finish_kernel, seq=seq, n_cast=len(names)),
        grid=(batch, n_i),
        in_specs=in_specs,
        out_specs=out_specs,
        out_shape=out_shape,
        compiler_params=pltpu.CompilerParams(dimension_semantics=("arbitrary", "arbitrary"),
                                             vmem_limit_bytes=VMEM_LIMIT),
        name="finish",
    )(*args)
    if nxt is None:
        return outs[0], None, None
    return outs[0], outs[1], dict(zip(names, outs[2:]))


def _lambda_init(layer):
    return 0.8 - 0.6 * math.exp(-0.3 * layer)


def _prep_weights(kind, p):
    w = dict(w_in=p["w_in"], w_out=p["w_out"])
    if kind == "mla":
        w_in = p["w_in"]
        d = w_in.shape[0]
        r2 = ML_Q_RANK + ML_KV_RANK
        w["w_in"] = jnp.concatenate(
            [w_in[:, :r2 + ML_ROPE_DIM], jnp.zeros((d, ML_Q_RANK - ML_ROPE_DIM), F32),
             w_in[:, r2 + ML_ROPE_DIM:]], axis=1)
        qd = ML_NOPE_DIM + ML_ROPE_DIM
        wq = p["w_q_b"].reshape(ML_Q_RANK, ML_HEADS, qd)
        wq = jnp.pad(wq, ((0, 0), (0, 0), (0, 2 * LANES - qd)))
        w["wq"] = wq.reshape(ML_Q_RANK, ML_HEADS * 2 * LANES)
        wkv = p["w_kv_b"].reshape(ML_KV_RANK, ML_HEADS, ML_NOPE_DIM + ML_V_DIM)
        w["wkv"] = jnp.concatenate([wkv[:, :, :ML_NOPE_DIM].reshape(ML_KV_RANK, -1),
                                    wkv[:, :, ML_NOPE_DIM:].reshape(ML_KV_RANK, -1)], axis=1)
    return w


def _layer(kind, layer, h3, mod, wts, p, nxt, *, seq, ctx_len):
    batch, t, d = h3.shape
    m = batch * t
    need_ctx = nxt is not None
    mod3 = mod.reshape(MOD_ROWS, 1, 3 * d)
    h2 = h3.reshape(m, d)
    one = jnp.ones((1, LANES), F32)
    common = dict(batch=batch, seq=seq, ctx_len=ctx_len)
    tables_t = functools.partial(_rope_tables_t, seq, ctx_len)

    def b3(a):
        return a.reshape(batch, t, a.shape[-1])

    if kind == "diff":
        tabs = (tables_t(DA_HEAD_DIM, None, DA_HEAD_DIM ** -0.5 * LOG2_E)
                + tables_t(DA_HEAD_DIM, None, 1.0))
        qt, k, vt, g = _proj_call(h2, mod3, p["pre_g"], wts["w_in"], tabs, one, one,
                                  kinds=["da_q"] * 2 + ["da_k"] * 2 + ["v"] * 2 + ["g"] * 2,
                                  tn=1024, o_cols=1024, v_rows=1024, tabs_transposed=True,
                                  rope_half=DA_HEAD_DIM // 4, **common)
        small = [p[n].reshape(1, DA_HEAD_DIM) for n in ("lam_q1", "lam_k1", "lam_q2", "lam_k2")]
        small.append(p["subln_g"].reshape(1, 2 * DA_HEAD_DIM))
        a3 = _attn_call("diff", qt, b3(k), vt, b3(g), small, seq=seq, need_ctx=need_ctx,
                        lambda_init=_lambda_init(layer))
    elif kind == "gqa":
        tabs = (tables_t(GQ_HEAD_DIM, p["q_norm_g"], GQ_HEAD_DIM ** -0.5 * LOG2_E)
                + tables_t(GQ_HEAD_DIM, p["k_norm_g"], 1.0))
        qt, k, vt, g = _proj_call(h2, mod3, p["pre_g"], wts["w_in"], tabs, one, one,
                                  kinds=["gq_q"] * 2 + ["gq_kv"] + ["g"] * 2,
                                  tn=1024, o_cols=GQ_KV_W, v_rows=GQ_KV_W, tabs_transposed=True,
                                  rope_half=GQ_HEAD_DIM // 4, **common)
        a3 = _attn_call("gqa", qt, b3(k), vt, b3(g), [], seq=seq, need_ctx=need_ctx)
    elif kind == "mla":
        tn = ML_Q_RANK
        qkv, g = _proj_call(h2, mod3, p["pre_g"], wts["w_in"],
                            _rope_tables(seq, ctx_len, ML_ROPE_DIM),
                            p["q_a_norm_g"].reshape(1, ML_Q_RANK),
                            p["kv_a_norm_g"].reshape(1, ML_KV_RANK),
                            kinds=["ml_qa", "ml_kva", "ml_kr"] + ["g"] * (d // tn),
                            tn=tn, o_cols=tn, v_rows=0, tabs_transposed=False,
                            rope_half=ML_ROPE_DIM // 4, **common)
        qd = ML_NOPE_DIM + ML_ROPE_DIM
        q_scale = qd ** -0.5 * LOG2_E
        qt, k, vt = _mla_b_call(qkv, wts["wq"], wts["wkv"], tables_t(ML_ROPE_DIM, None, q_scale),
                                seq=seq, ctx_len=ctx_len, q_scale=q_scale)
        a3 = _attn_call("mla", qt, b3(k), vt, b3(g), [], seq=seq, need_ctx=need_ctx)
    else:
        raise ValueError(kind)
    return _finish_call(a3, h3, mod3, p["post_g"], wts["w_out"], seq=seq, nxt=nxt)


def kernel(x, c, ctx, c_ctx, l0_ada_w, l0_ada_b, l0_pre_g, l0_post_g, l0_w_in, l0_lam_q1, l0_lam_k1, l0_lam_q2, l0_lam_k2, l0_subln_g, l0_w_out, l1_ada_w, l1_ada_b, l1_pre_g, l1_post_g, l1_w_in, l1_q_norm_g, l1_k_norm_g, l1_w_out, l2_ada_w, l2_ada_b, l2_pre_g, l2_post_g, l2_w_in, l2_q_a_norm_g, l2_w_q_b, l2_kv_a_norm_g, l2_w_kv_b, l2_w_out, l3_ada_w, l3_ada_b, l3_pre_g, l3_post_g, l3_w_in, l3_lam_q1, l3_lam_k1, l3_lam_q2, l3_lam_k2, l3_subln_g, l3_w_out):
    batch, seq, d = x.shape
    ctx_len = ctx.shape[1]
    assert d == D_MODEL and seq % GRID_W == 0 and batch + 1 <= MOD_ROWS
    diff_names = ("ada_w", "ada_b", "pre_g", "post_g", "w_in", "lam_q1", "lam_k1", "lam_q2",
                  "lam_k2", "subln_g", "w_out")
    layers = [
        ("diff", dict(zip(diff_names, (l0_ada_w, l0_ada_b, l0_pre_g, l0_post_g, l0_w_in,
                                       l0_lam_q1, l0_lam_k1, l0_lam_q2, l0_lam_k2, l0_subln_g,
                                       l0_w_out)))),
        ("gqa", dict(ada_w=l1_ada_w, ada_b=l1_ada_b, pre_g=l1_pre_g, post_g=l1_post_g,
                     w_in=l1_w_in, q_norm_g=l1_q_norm_g, k_norm_g=l1_k_norm_g, w_out=l1_w_out)),
        ("mla", dict(ada_w=l2_ada_w, ada_b=l2_ada_b, pre_g=l2_pre_g, post_g=l2_post_g,
                     w_in=l2_w_in, q_a_norm_g=l2_q_a_norm_g, w_q_b=l2_w_q_b,
                     kv_a_norm_g=l2_kv_a_norm_g, w_kv_b=l2_w_kv_b, w_out=l2_w_out)),
        ("diff", dict(zip(diff_names, (l3_ada_w, l3_ada_b, l3_pre_g, l3_post_g, l3_w_in,
                                       l3_lam_q1, l3_lam_k1, l3_lam_q2, l3_lam_k2, l3_subln_g,
                                       l3_w_out)))),
    ]
    cc = jnp.zeros((MOD_ROWS, d), F32).at[:batch].set(c).at[batch].set(c_ctx)
    h3 = jnp.concatenate([x, ctx], axis=1)
    kind0, p0 = layers[0]
    mod = _ada_call(cc, p0["ada_w"], p0["ada_b"])
    wts = {name: w.astype(BF16) for name, w in _prep_weights(kind0, p0).items()}
    for layer, (kind, p) in enumerate(layers):
        nxt = None
        if layer + 1 < len(layers):
            kind_n, p_n = layers[layer + 1]
            nxt = (cc, p_n["ada_w"], p_n["ada_b"], _prep_weights(kind_n, p_n))
        h3, mod, wts = _layer(kind, layer, h3, mod, wts, p, nxt, seq=seq, ctx_len=ctx_len)
    return h3
```

```python
import functools
import math

import jax
import jax.numpy as jnp
import numpy as np
from jax import lax
from jax.experimental import pallas as pl
from jax.experimental.pallas import tpu as pltpu

F32 = jnp.float32
BF16 = jnp.bfloat16
LOG2_E = math.log2(math.e)

D_MODEL = 2048
GRID_W = 64
ROPE_BASE = 10000.0
NORM_EPS = 1e-6
DA_HEAD_DIM = 64
DA_HEADS = D_MODEL // (2 * DA_HEAD_DIM)
GQ_HEAD_DIM = 128
GQ_HEADS = D_MODEL // GQ_HEAD_DIM
GQ_KV_HEADS = GQ_HEADS // 4
GQ_KV_W = GQ_KV_HEADS * GQ_HEAD_DIM
ML_NOPE_DIM = 128
ML_ROPE_DIM = 64
ML_V_DIM = 128
ML_HEADS = D_MODEL // ML_V_DIM
ML_Q_RANK = D_MODEL // 4
ML_KV_RANK = D_MODEL // 4

LANES = 128
SUBLANES = 8
MXU_N = 512
ROW_CHUNK = 128
VMEM_LIMIT = 56 * 1024 * 1024
MOD_ROWS = 16

PROJ_TM = 768
FINISH_TM = 384
FINISH_TM_LAST = 512
PROJ_SUB = 256
FINISH_SUB = 128
MLA_HEADS_PER_STEP = 2
DA_HEADS_PER_STEP = 2
KEY_CHUNK = 256


def _silu(x):
    return x / (1.0 + jnp.exp(-x))


def _rms(x, eps=NORM_EPS):
    return x * lax.rsqrt(jnp.mean(x * x, axis=-1, keepdims=True) + eps)


def _rope(x, cos, sin_a, sin_b, half):
    return (x * cos + pltpu.roll(x, LANES - half, 1) * sin_a + pltpu.roll(x, half, 1) * sin_b)


def _swap_halves(x, half):
    parts = []
    for r in range(0, x.shape[0], 2 * half):
        parts += [x[r + half:r + 2 * half], x[r:r + half]]
    return jnp.concatenate(parts, axis=0)


def _rope_t(x, cos_t, sin_t, half):
    return x * cos_t + _swap_halves(x, half) * sin_t


def _rope_tables(seq, ctx_len, head_dim):
    t = seq + ctx_len
    r = np.arange(t)
    lat = (r < seq)[:, None]
    rowpos = (r // GRID_W).astype(np.float32)[:, None]
    colpos = (r % GRID_W).astype(np.float32)[:, None]
    lane = np.arange(LANES)
    u = lane % head_dim
    half2 = head_dim // 2
    half = half2 // 2
    grp = (u // half2)[None, :]
    w = u % half2
    first = (w < half)[None, :]
    inv_freq = (np.float32(ROPE_BASE) ** (-(w % half).astype(np.float32) / np.float32(half)))[None, :]
    ang = (np.where(grp == 0, rowpos, colpos) * inv_freq).astype(np.float32)
    cos = np.where(lat, np.cos(ang), 1.0).astype(np.float32)
    sin = np.where(lat, np.sin(ang), 0.0).astype(np.float32)
    sin_a = np.where(first, -sin, np.float32(0.0))
    sin_b = np.where(first, np.float32(0.0), sin)
    return cos, sin_a, sin_b


def _rope_tables_t(seq, ctx_len, head_dim, gain, scale):
    cos, sin_a, sin_b = _rope_tables(seq, ctx_len, head_dim)
    half = head_dim // 4
    lane = np.arange(LANES)
    partner = np.where((lane % (2 * half)) < half, lane + half, lane - half)
    cos_t = jnp.asarray(np.ascontiguousarray((cos * np.float32(scale)).T))
    sin_t = jnp.asarray(np.ascontiguousarray(((sin_a + sin_b) * np.float32(scale)).T))
    if gain is None:
        return cos_t, sin_t
    g = jnp.tile(gain.astype(F32), LANES // head_dim)
    return cos_t * g[:, None], sin_t * g[partner][:, None]


def _ada_kernel(c_ref, w_ref, b_ref, o_ref):
    s = _silu(c_ref[...])
    o_ref[...] = jnp.dot(s.astype(BF16), w_ref[...].astype(BF16),
                         preferred_element_type=F32) + b_ref[...]


def _ada_call(cc, ada_w, ada_b):
    d, n = ada_w.shape
    tn = 768
    return pl.pallas_call(
        _ada_kernel,
        grid=(n // tn,),
        in_specs=[pl.BlockSpec((MOD_ROWS, d), lambda j: (0, 0)),
                  pl.BlockSpec((d, tn), lambda j: (0, j)),
                  pl.BlockSpec((1, tn), lambda j: (0, j))],
        out_specs=pl.BlockSpec((MOD_ROWS, tn), lambda j: (0, j)),
        out_shape=jax.ShapeDtypeStruct((MOD_ROWS, n), F32),
        compiler_params=pltpu.CompilerParams(dimension_semantics=("arbitrary",),
                                             vmem_limit_bytes=VMEM_LIMIT),
        name="ada",
    )(cc, ada_w, ada_b.reshape(1, n))


_KIND_OUTS = {"da_q": ("qt",), "gq_q": ("qt",), "da_k": ("o",), "gq_kv": ("o", "vt"),
              "v": ("vt",), "g": ("g",), "ml_qa": ("o",), "ml_kva": ("o",), "ml_kr": ("o",)}


def _mod_rows(mod_b_ref, mod_c_ref, is_ctx, part):
    d = D_MODEL
    mb = mod_b_ref[:, part * d:(part + 1) * d]
    mc = mod_c_ref[:, part * d:(part + 1) * d]
    return jnp.where(is_ctx, mc, mb)


def _h_chunk(h_refs, n_ctx_refs, c, is_ctx):
    if len(h_refs) == 1:
        return h_refs[0][c * ROW_CHUNK:(c + 1) * ROW_CHUNK, :]
    n_sb = len(h_refs) - n_ctx_refs
    per_sb = h_refs[0].shape[0] // ROW_CHUNK
    r, rows = c // per_sb, slice((c % per_sb) * ROW_CHUNK, (c % per_sb + 1) * ROW_CHUNK)
    v = h_refs[r][rows, :]
    if r >= n_sb - n_ctx_refs:
        v = jnp.where(is_ctx, h_refs[n_sb + r - (n_sb - n_ctx_refs)][rows, :], v)
    return v


def _split_h_specs(x, ctx, tm, sub, idx):
    batch, seq, d = x.shape
    ctx_len = ctx.shape[1]
    assert tm % sub == 0 and seq % sub == 0 and ctx_len % sub == 0 and (seq + ctx_len) % tm == 0
    n_sb, last = tm // sub, seq // sub - 1
    xv = x.reshape(batch, seq // sub, sub, d)
    cv = ctx.reshape(batch, ctx_len // sub, sub, d)
    specs = [pl.BlockSpec((None, None, sub, d),
                          lambda *g, r=r: (idx(*g)[0], jnp.minimum(idx(*g)[1] * n_sb + r, last), 0, 0))
             for r in range(n_sb)]
    specs += [pl.BlockSpec((None, None, sub, d), lambda *g, q=q: (idx(*g)[0], q, 0, 0))
              for q in range(ctx_len // sub)]
    return [xv] * n_sb + [cv] * (ctx_len // sub), specs, ctx_len // sub


def _prologue(h_refs, n_ctx_refs, mod_b_ref, mod_c_ref, preg_ref, xn_ref, slot, row0, seq, pieces):
    preg = preg_ref[...]
    for c in pieces:
        rows = pl.ds(c * ROW_CHUNK, ROW_CHUNK)
        is_ctx = (row0 + c * ROW_CHUNK) >= seq
        shift = _mod_rows(mod_b_ref, mod_c_ref, is_ctx, 0)
        scale = _mod_rows(mod_b_ref, mod_c_ref, is_ctx, 1)
        y = _rms(_h_chunk(h_refs, n_ctx_refs, c, is_ctx)) * preg
        xn_ref[slot, rows, :] = (y * (1.0 + scale) + shift).astype(BF16)


def _proj_kernel(*refs, kinds, out_names, n_h, n_ctx_refs, n_tabs, seq, n_tiles, tiles_per_batch,
                 rope_half):
    h_refs, refs = refs[:n_h], refs[n_h:]
    mod_b_ref, mod_c_ref, preg_ref, w_ref = refs[:4]
    tabs = refs[4:4 + n_tabs]
    ng0_ref, ng1_ref = refs[4 + n_tabs:6 + n_tabs]
    outs = dict(zip(out_names, refs[6 + n_tabs:]))
    xn_ref = refs[-1]
    i = pl.program_id(0)
    j = pl.program_id(1)
    tm = xn_ref.shape[1]
    n_steps = len(kinds)
    n_pieces = tm // ROW_CHUNK
    row0 = (jnp.minimum(i, n_tiles - 1) % tiles_per_batch) * tm
    prologue = functools.partial(_prologue, h_refs, n_ctx_refs, mod_b_ref, mod_c_ref, preg_ref,
                                 xn_ref, i % 2, row0, seq)

    @pl.when((i == 0) & (j == 0))
    def _():
        prologue(range(n_pieces))

    def heads_t(a):
        at = a.T
        return [at[c * LANES:(c + 1) * LANES] for c in range(at.shape[0] // LANES)]

    def head_rms(xh):
        return lax.rsqrt(jnp.mean(xh * xh, axis=0, keepdims=True) + NORM_EPS)

    def tile(kind):
        a = jnp.dot(xn_ref[(i + 1) % 2], w_ref[...], preferred_element_type=F32)
        if kind == "g":
            outs["g"][...] = a
        elif kind == "v":
            outs["vt"][...] = a.T.astype(BF16)
        elif kind in ("da_q", "gq_q"):
            cos_t, sin_t = tabs[0][...], tabs[1][...]
            for c, xh in enumerate(heads_t(a)):
                y = _rope_t(xh, cos_t, sin_t, rope_half)
                if kind == "gq_q":
                    y = y * head_rms(xh)
                outs["qt"][c * LANES:(c + 1) * LANES, :] = y.astype(BF16)
        elif kind in ("da_k", "gq_kv"):
            cos_t, sin_t = tabs[2][...], tabs[3][...]
            k_cols = outs["o"].shape[1]
            for c, xh in enumerate(heads_t(a[:, :k_cols])):
                y = _rope_t(xh, cos_t, sin_t, rope_half)
                if kind == "gq_kv":
                    y = y * head_rms(xh)
                outs["o"][:, c * LANES:(c + 1) * LANES] = y.T.astype(BF16)
            if kind == "gq_kv":
                outs["vt"][...] = a[:, k_cols:].T.astype(BF16)
        elif kind == "ml_qa":
            outs["o"][...] = (_rms(a) * ng0_ref[...]).astype(BF16)
        elif kind == "ml_kva":
            outs["o"][...] = (_rms(a) * ng1_ref[...]).astype(BF16)
        elif kind == "ml_kr":
            outs["o"][...] = a.astype(BF16)
            outs["o"][:, :LANES] = _rope(a[:, :LANES], tabs[0][...], tabs[1][...], tabs[2][...],
                                         rope_half).astype(BF16)
        else:
            raise ValueError(kind)

    def step(jv):
        tile(kinds[jv])
        prologue([p for p in range(n_pieces) if n_steps - 1 - p % n_steps == jv])

    for jv in range(n_steps):
        pl.when((i > 0) & (j == jv))(functools.partial(step, jv))


def _proj_call(h, mod3, pre_g, w, tabs, ng0, ng1, *, kinds, tn, o_cols, v_rows, tabs_transposed,
               batch, seq, ctx_len, rope_half):
    d = D_MODEL
    m = batch * (seq + ctx_len)
    n = w.shape[1]
    t = seq + ctx_len
    tm = PROJ_TM
    tpb = t // tm
    n_tiles = m // tm
    assert t % tm == 0 and n % tn == 0 and len(kinds) == n // tn

    def nxt(i):
        return jnp.minimum(i, n_tiles - 1)

    def cur(i):
        return jnp.maximum(i - 1, 0)

    def tile_idx(name):
        js = [jv for jv, k in enumerate(kinds) if name in _KIND_OUTS[k]]
        assert js == list(range(js[0], js[-1] + 1))
        return lambda i, j: jnp.where(i == 0, 0, jnp.clip(j - js[0], 0, len(js) - 1)), len(js)

    out_names, out_specs, out_shape = [], [], []
    for name in ("qt", "o", "vt", "g"):
        if not any(name in _KIND_OUTS[k] for k in kinds):
            continue
        idx, cnt = tile_idx(name)
        out_names.append(name)
        if name in ("qt", "vt"):
            rows = tn if name == "qt" else v_rows
            out_specs.append(pl.BlockSpec((rows, tm), lambda i, j, idx=idx: (idx(i, j), cur(i))))
            out_shape.append(jax.ShapeDtypeStruct((cnt * rows, m), BF16))
        else:
            cols, dt = (o_cols, BF16) if name == "o" else (tn, F32)
            out_specs.append(pl.BlockSpec((tm, cols), lambda i, j, idx=idx: (cur(i), idx(i, j))))
            out_shape.append(jax.ShapeDtypeStruct((m, cnt * cols), dt))
    assert out_shape[-1].shape == (m, d)

    if tabs_transposed:
        tab_spec = pl.BlockSpec((LANES, tm), lambda i, j: (0, cur(i) % tpb))
    else:
        tab_spec = pl.BlockSpec((tm, LANES), lambda i, j: (cur(i) % tpb, 0))
    if isinstance(h, tuple):
        h_args, h_specs, n_ctx_refs = _split_h_specs(
            *h, tm, PROJ_SUB, lambda i, j: (nxt(i) // tpb, nxt(i) % tpb))
    else:
        h_args, h_specs, n_ctx_refs = [h], [pl.BlockSpec((tm, d), lambda i, j: (nxt(i), 0))], 0
    kern = functools.partial(_proj_kernel, kinds=tuple(kinds), out_names=tuple(out_names),
                             n_h=len(h_args), n_ctx_refs=n_ctx_refs, n_tabs=len(tabs), seq=seq,
                             n_tiles=n_tiles, tiles_per_batch=tpb, rope_half=rope_half)
    return pl.pallas_call(
        kern,
        grid=(n_tiles + 1, n // tn),
        in_specs=h_specs + [
            pl.BlockSpec((None, 1, 3 * d), lambda i, j: (nxt(i) // tpb, 0, 0)),
            pl.BlockSpec((None, 1, 3 * d), lambda i, j: (batch, 0, 0)),
            pl.BlockSpec((1, d), lambda i, j: (0, 0)),
            pl.BlockSpec((d, tn), lambda i, j: (0, jnp.where(i == 0, 0, j))),
        ] + [tab_spec] * len(tabs) + [
            pl.BlockSpec(ng0.shape, lambda i, j: (0, 0)),
            pl.BlockSpec(ng1.shape, lambda i, j: (0, 0)),
        ],
        out_specs=out_specs,
        out_shape=out_shape,
        scratch_shapes=[pltpu.VMEM((2, tm, d), BF16)],
        compiler_params=pltpu.CompilerParams(dimension_semantics=("arbitrary", "arbitrary"),
                                             vmem_limit_bytes=VMEM_LIMIT),
        name="proj",
    )(*h_args, mod3, mod3, pre_g.reshape(1, d), w, *tabs, ng0, ng1)


def _mla_b_kernel(qa_ref, kva_ref, kr_ref, wq_ref, wkn_ref, wv_ref, cos_t_ref, sin_t_ref,
                  qt_ref, k_ref, vt_ref, *, q_scale):
    qt = jnp.dot(qa_ref[...], wq_ref[...], preferred_element_type=F32).T
    for c in range(qt.shape[0] // LANES):
        x = qt[c * LANES:(c + 1) * LANES]
        if c % 2 == 1:
            x = _rope_t(x, cos_t_ref[...], sin_t_ref[...], ML_ROPE_DIM // 4)
        else:
            x = x * q_scale
        qt_ref[c * LANES:(c + 1) * LANES, :] = x.astype(BF16)
    kva = kva_ref[...]
    kn = jnp.dot(kva, wkn_ref[...], preferred_element_type=F32)
    kr = kr_ref[...]
    for hh in range(kn.shape[1] // LANES):
        k_ref[:, 2 * hh * LANES:(2 * hh + 1) * LANES] = kn[:, hh * LANES:(hh + 1) * LANES].astype(BF16)
        k_ref[:, (2 * hh + 1) * LANES:(2 * hh + 2) * LANES] = kr
    vt_ref[...] = jnp.dot(kva, wv_ref[...], preferred_element_type=F32).T.astype(BF16)


def _mla_b_call(qkv, wq, wkv, tabs_t, *, seq, ctx_len, q_scale):
    m = qkv.shape[0]
    tm = PROJ_TM
    tpb = (seq + ctx_len) // tm
    r = ML_Q_RANK
    nq = wq.shape[1]
    nkv = wkv.shape[1] // 2
    steps = 4
    tq, tk = nq // steps, nkv // steps
    kr_col = (ML_Q_RANK + ML_KV_RANK) // LANES
    tab_spec = pl.BlockSpec((LANES, tm), lambda i, j: (0, i % tpb))
    return pl.pallas_call(
        functools.partial(_mla_b_kernel, q_scale=q_scale),
        grid=(m // tm, steps),
        in_specs=[
            pl.BlockSpec((tm, r), lambda i, j: (i, 0)),
            pl.BlockSpec((tm, r), lambda i, j: (i, 1)),
            pl.BlockSpec((tm, LANES), lambda i, j: (i, kr_col)),
            pl.BlockSpec((r, tq), lambda i, j: (0, j)),
            pl.BlockSpec((r, tk), lambda i, j: (0, j)),
            pl.BlockSpec((r, tk), lambda i, j: (0, steps + j)),
            tab_spec, tab_spec,
        ],
        out_specs=[
            pl.BlockSpec((tq, tm), lambda i, j: (j, i)),
            pl.BlockSpec((tm, 2 * tk), lambda i, j: (i, j)),
            pl.BlockSpec((tk, tm), lambda i, j: (j, i)),
        ],
        out_shape=[jax.ShapeDtypeStruct((nq, m), BF16),
                   jax.ShapeDtypeStruct((m, 2 * nkv), BF16),
                   jax.ShapeDtypeStruct((nkv, m), BF16)],
        compiler_params=pltpu.CompilerParams(dimension_semantics=("arbitrary", "arbitrary"),
                                             vmem_limit_bytes=VMEM_LIMIT),
        name="mla_b",
    )(qkv, qkv, qkv, wq, wkv, wkv, *tabs_t)


def _sublane_groups(x):
    return x.reshape(x.shape[0] // SUBLANES, SUBLANES, x.shape[1])


def _run_chains(chains, s_scr, e_scr):
    n = len(chains)
    shape, m, l = {}, {}, {}
    for step in range(n + 2):
        if step < n:
            qt, k, vt, _ = chains[step]
            qv = qt()
            nq, nk = qv.shape[1], vt().shape[1]
            shape[step] = (nk, nq)
            kc = min(KEY_CHUNK, nk)
            m8 = None
            for c0 in range(0, nk, kc):
                st = jnp.dot(k(c0, kc), qv, preferred_element_type=F32)
                s_scr[step % 2, c0:c0 + kc, :nq] = st
                cm = jnp.max(_sublane_groups(st), axis=0)
                m8 = cm if m8 is None else jnp.maximum(m8, cm)
            m[step] = jnp.max(m8, axis=0, keepdims=True)
        i = step - 1
        if 0 <= i < n:
            nk, nq = shape[i]
            mi = m.pop(i)
            kc = min(KEY_CHUNK, nk)
            l8 = None
            for c0 in range(0, nk, kc):
                p = jnp.exp2(s_scr[i % 2, c0:c0 + kc, :nq] - mi)
                ps = jnp.sum(_sublane_groups(p), axis=0)
                l8 = ps if l8 is None else l8 + ps
                e_scr[i % 2, c0:c0 + kc, :nq] = p.astype(BF16)
            l[i] = jnp.sum(l8, axis=0, keepdims=True)
        i = step - 2
        if 0 <= i < n:
            nk, nq = shape[i]
            _, _, vt, done = chains[i]
            done(jnp.dot(vt(), e_scr[i % 2, :nk, :nq], preferred_element_type=F32) / l.pop(i))


def _row_blocks(rows_q, seq):
    assert seq % MXU_N == 0
    blocks = [(r0, MXU_N, 0) for r0 in range(0, seq, MXU_N)]
    if rows_q > seq:
        blocks.append((seq, rows_q - seq, seq))
    return blocks


def _diff_attn_kernel(lq1_ref, lk1_ref, lq2_ref, lk2_ref, subg_ref, qt_ref, k_ref, vt_ref, g_ref,
                      o_ref, s_scr, e_scr, *, seq, lambda_init):
    lam = (jnp.exp(jnp.sum(lq1_ref[...] * lk1_ref[...], axis=-1, keepdims=True))
           - jnp.exp(jnp.sum(lq2_ref[...] * lk2_ref[...], axis=-1, keepdims=True))
           + lambda_init)
    chains = []
    for r0, nr, key0 in _row_blocks(o_ref.shape[0], seq):
        rows = slice(r0, r0 + nr)
        for hh in range(qt_ref.shape[0] // LANES):
            cols = slice(hh * LANES, (hh + 1) * LANES)
            parts = []

            def q_sub(sub, rows=rows, cols=cols):
                qf = qt_ref[cols, rows].astype(F32)
                row = lax.broadcasted_iota(jnp.int32, qf.shape, 0)
                return jnp.where((row >= DA_HEAD_DIM) == bool(sub), qf, 0.0).astype(BF16)

            def done(ot, rows=rows, cols=cols, parts=parts):
                parts.append(ot)
                if len(parts) == 2:
                    o = (parts[0] - lam * parts[1]).T
                    o = _rms(o) * subg_ref[...] * (1.0 - lambda_init)
                    o_ref[rows, cols] = (o * _silu(g_ref[rows, cols])).astype(BF16)

            for sub in range(2):
                chains.append((functools.partial(q_sub, sub),
                               lambda c0, kc, key0=key0, cols=cols: k_ref[key0 + c0:key0 + c0 + kc,
                                                                          cols],
                               lambda key0=key0, cols=cols: vt_ref[cols, key0:], done))
    _run_chains(chains, s_scr, e_scr)


def _plain_attn_kernel(qt_ref, k_ref, vt_ref, g_ref, o_ref, s_scr, e_scr, *, seq, dq, shared_kv):
    n_heads = qt_ref.shape[0] // dq
    chains = []
    for r0, nr, key0 in _row_blocks(o_ref.shape[0], seq):
        rows = slice(r0, r0 + nr)
        for hh in range(n_heads):
            kh = 0 if shared_kv else hh
            cols = slice(hh * LANES, (hh + 1) * LANES)

            def done(ot, rows=rows, cols=cols):
                o_ref[rows, cols] = (ot.T * _silu(g_ref[rows, cols])).astype(BF16)

            chains.append((lambda rows=rows, hh=hh: qt_ref[hh * dq:(hh + 1) * dq, rows],
                           lambda c0, kc, key0=key0, kh=kh: k_ref[key0 + c0:key0 + c0 + kc,
                                                                  kh * dq:(kh + 1) * dq],
                           lambda key0=key0, kh=kh: vt_ref[kh * LANES:(kh + 1) * LANES, key0:],
                           done))
    _run_chains(chains, s_scr, e_scr)


def _attn_call(kind, qt, k3, vt, g3, small, *, seq, need_ctx, lambda_init=None):
    batch, t, _ = k3.shape
    rows = t if need_ctx else seq
    heads = D_MODEL // LANES
    if kind == "diff":
        hps, kv_heads, dq, k_col0 = DA_HEADS_PER_STEP, DA_HEADS_PER_STEP, LANES, 0
        kern = functools.partial(_diff_attn_kernel, seq=seq, lambda_init=lambda_init)
    elif kind == "gqa":
        hps, kv_heads, dq, k_col0 = GQ_HEADS // GQ_KV_HEADS, 1, LANES, 0
        kern = functools.partial(_plain_attn_kernel, seq=seq, dq=dq, shared_kv=True)
    elif kind == "mla":
        hps, kv_heads, dq, k_col0 = MLA_HEADS_PER_STEP, MLA_HEADS_PER_STEP, 2 * LANES, 0
        kern = functools.partial(_plain_attn_kernel, seq=seq, dq=dq, shared_kv=False)
    else:
        raise ValueError(kind)
    kw = kv_heads * dq
    in_specs = [pl.BlockSpec(a.shape, lambda b, s: (0,) * a.ndim) for a in small] + [
        pl.BlockSpec((hps * dq, t), lambda b, s: (s, b)),
        pl.BlockSpec((None, t, kw), lambda b, s: (b, 0, k_col0 + s)),
        pl.BlockSpec((kv_heads * LANES, t), lambda b, s: (s, b)),
        pl.BlockSpec((None, rows, hps * LANES), lambda b, s: (b, 0, s)),
    ]
    return pl.pallas_call(
        kern,
        grid=(batch, heads // hps),
        in_specs=in_specs,
        out_specs=pl.BlockSpec((None, rows, hps * LANES), lambda b, s: (b, 0, s)),
        out_shape=jax.ShapeDtypeStruct((batch, rows, D_MODEL), BF16),
        scratch_shapes=[pltpu.VMEM((2, t, MXU_N), F32), pltpu.VMEM((2, t, MXU_N), BF16)],
        compiler_params=pltpu.CompilerParams(dimension_semantics=("arbitrary", "arbitrary"),
                                             vmem_limit_bytes=VMEM_LIMIT),
        name="attn_" + kind,
    )(*small, qt, k3, vt, g3)


def _finish_kernel(*refs, seq, n_h, n_ctx_refs, n_cast):
    h_refs, refs = refs[:n_h], refs[n_h:]
    a_ref, mod_b_ref, mod_c_ref, postg_ref, w_ref = refs[:5]
    n_in = 5 + (3 + n_cast if n_cast else 0)
    o_ref = refs[n_in]
    ti = pl.program_id(1)
    tm = a_ref.shape[0]
    y = jnp.dot(a_ref[...], w_ref[...], preferred_element_type=F32)
    postg = postg_ref[...]
    for c in range(tm // ROW_CHUNK):
        rows = slice(c * ROW_CHUNK, (c + 1) * ROW_CHUNK)
        is_ctx = (ti * tm + c * ROW_CHUNK) >= seq
        gate = _mod_rows(mod_b_ref, mod_c_ref, is_ctx, 2)
        o_ref[rows, :] = (_h_chunk(h_refs, n_ctx_refs, c, is_ctx)
                          + gate * (_rms(y[rows, :]) * postg))
    if n_cast:
        c_ref, adaw_ref, adab_ref = refs[5:8]
        _ada_kernel(c_ref, adaw_ref, adab_ref, refs[n_in + 1])
        for src, dst in zip(refs[8:n_in], refs[n_in + 2:]):
            dst[...] = src[...].astype(BF16)


def _finish_call(a3, h, mod3, post_g, w, *, seq, nxt=None):
    batch, rows_out, d = a3.shape
    t = sum(v.shape[1] for v in h) if isinstance(h, tuple) else h.shape[1]
    tm = FINISH_TM if rows_out == t else FINISH_TM_LAST
    assert rows_out % tm == 0
    n_i = rows_out // tm
    blk = pl.BlockSpec((None, tm, d), lambda b, i: (b, i, 0))
    if isinstance(h, tuple):
        h_args, h_specs, n_ctx_refs = _split_h_specs(*h, tm, FINISH_SUB, lambda b, i: (b, i))
    else:
        h_args, h_specs, n_ctx_refs = [h], [blk], 0
    in_specs = h_specs + [
        blk,
        pl.BlockSpec((None, 1, 3 * d), lambda b, i: (b, 0, 0)),
        pl.BlockSpec((None, 1, 3 * d), lambda b, i: (batch, 0, 0)),
        pl.BlockSpec((1, d), lambda b, i: (0, 0)),
        pl.BlockSpec((d, d), lambda b, i: (0, 0)),
    ]
    args = h_args + [a3, mod3, mod3, post_g.reshape(1, d), w]
    out_specs = [blk]
    out_shape = [jax.ShapeDtypeStruct((batch, rows_out, d), F32)]
    names = []
    if nxt is not None:
        cc, ada_w, ada_b, weights = nxt
        n_mod = ada_w.shape[1]
        assert n_mod == batch * n_i * LANES
        flat = lambda b, i: (0, b * n_i + i)
        in_specs += [pl.BlockSpec((MOD_ROWS, d), lambda b, i: (0, 0)),
                     pl.BlockSpec((d, LANES), flat), pl.BlockSpec((1, LANES), flat)]
        args += [cc, ada_w, ada_b.reshape(1, n_mod)]
        out_specs.append(pl.BlockSpec((MOD_ROWS, LANES), flat))
        out_shape.append(jax.ShapeDtypeStruct((MOD_ROWS, n_mod), F32))
        for name, wf in weights.items():
            r, c = wf.shape
            assert r % (batch * 2 * SUBLANES) == 0 and c % (4 * LANES) == 0 and n_i >= 4
            spec = pl.BlockSpec((r // batch, c // 4), lambda b, i: (b, jnp.minimum(i, 3)))
            names.append(name)
            in_specs.append(spec)
            args.append(wf)
            out_specs.append(spec)
            out_shape.append(jax.ShapeDtypeStruct((r, c), BF16))
    outs = pl.pallas_call(
        functools.partial(_finish_kernel, seq=seq, n_h=len(h_args), n_ctx_refs=n_ctx_refs,
                          n_cast=len(names)),
        grid=(batch, n_i),
        in_specs=in_specs,
        out_specs=out_specs,
        out_shape=out_shape,
        compiler_params=pltpu.CompilerParams(dimension_semantics=("arbitrary", "arbitrary"),
                                             vmem_limit_bytes=VMEM_LIMIT),
        name="finish",
    )(*args)
    if nxt is None:
        return outs[0], None, None
    return outs[0], outs[1], dict(zip(names, outs[2:]))


def _lambda_init(layer):
    return 0.8 - 0.6 * math.exp(-0.3 * layer)


def _prep_weights(kind, p):
    if kind != "mla":
        return dict(w_in=p["w_in"], w_out=p["w_out"]), {}
    w_in = p["w_in"].astype(BF16)
    d = w_in.shape[0]
    r2 = ML_Q_RANK + ML_KV_RANK
    ready = {}
    ready["w_in"] = jnp.concatenate(
        [w_in[:, :r2 + ML_ROPE_DIM], jnp.zeros((d, ML_Q_RANK - ML_ROPE_DIM), BF16),
         w_in[:, r2 + ML_ROPE_DIM:]], axis=1)
    qd = ML_NOPE_DIM + ML_ROPE_DIM
    wq = p["w_q_b"].astype(BF16).reshape(ML_Q_RANK, ML_HEADS, qd)
    wq = jnp.pad(wq, ((0, 0), (0, 0), (0, 2 * LANES - qd)))
    ready["wq"] = wq.reshape(ML_Q_RANK, ML_HEADS * 2 * LANES)
    wkv = p["w_kv_b"].astype(BF16).reshape(ML_KV_RANK, ML_HEADS, ML_NOPE_DIM + ML_V_DIM)
    ready["wkv"] = jnp.concatenate([wkv[:, :, :ML_NOPE_DIM].reshape(ML_KV_RANK, -1),
                                    wkv[:, :, ML_NOPE_DIM:].reshape(ML_KV_RANK, -1)], axis=1)
    return dict(w_out=p["w_out"]), ready


def _layer(kind, layer, h3, mod, wts, p, nxt, *, seq, ctx_len):
    d = D_MODEL
    t = seq + ctx_len
    batch = h3[0].shape[0] if isinstance(h3, tuple) else h3.shape[0]
    need_ctx = nxt is not None
    mod3 = mod.reshape(MOD_ROWS, 1, 3 * d)
    h2 = h3 if isinstance(h3, tuple) else h3.reshape(batch * t, d)
    one = jnp.ones((1, LANES), F32)
    common = dict(batch=batch, seq=seq, ctx_len=ctx_len)
    tables_t = functools.partial(_rope_tables_t, seq, ctx_len)

    def b3(a):
        return a.reshape(batch, t, a.shape[-1])

    if kind == "diff":
        tabs = (tables_t(DA_HEAD_DIM, None, DA_HEAD_DIM ** -0.5 * LOG2_E)
                + tables_t(DA_HEAD_DIM, None, 1.0))
        qt, k, vt, g = _proj_call(h2, mod3, p["pre_g"], wts["w_in"], tabs, one, one,
                                  kinds=["da_q"] * 2 + ["da_k"] * 2 + ["v"] * 2 + ["g"] * 2,
                                  tn=1024, o_cols=1024, v_rows=1024, tabs_transposed=True,
                                  rope_half=DA_HEAD_DIM // 4, **common)
        small = [p[n].reshape(1, DA_HEAD_DIM) for n in ("lam_q1", "lam_k1", "lam_q2", "lam_k2")]
        small.append(p["subln_g"].reshape(1, 2 * DA_HEAD_DIM))
        a3 = _attn_call("diff", qt, b3(k), vt, b3(g), small, seq=seq, need_ctx=need_ctx,
                        lambda_init=_lambda_init(layer))
    elif kind == "gqa":
        tabs = (tables_t(GQ_HEAD_DIM, p["q_norm_g"], GQ_HEAD_DIM ** -0.5 * LOG2_E)
                + tables_t(GQ_HEAD_DIM, p["k_norm_g"], 1.0))
        qt, k, vt, g = _proj_call(h2, mod3, p["pre_g"], wts["w_in"], tabs, one, one,
                                  kinds=["gq_q"] * 2 + ["gq_kv"] + ["g"] * 2,
                                  tn=1024, o_cols=GQ_KV_W, v_rows=GQ_KV_W, tabs_transposed=True,
                                  rope_half=GQ_HEAD_DIM // 4, **common)
        a3 = _attn_call("gqa", qt, b3(k), vt, b3(g), [], seq=seq, need_ctx=need_ctx)
    elif kind == "mla":
        tn = ML_Q_RANK
        qkv, g = _proj_call(h2, mod3, p["pre_g"], wts["w_in"],
                            _rope_tables(seq, ctx_len, ML_ROPE_DIM),
                            p["q_a_norm_g"].reshape(1, ML_Q_RANK),
                            p["kv_a_norm_g"].reshape(1, ML_KV_RANK),
                            kinds=["ml_qa", "ml_kva", "ml_kr"] + ["g"] * (d // tn),
                            tn=tn, o_cols=tn, v_rows=0, tabs_transposed=False,
                            rope_half=ML_ROPE_DIM // 4, **common)
        qd = ML_NOPE_DIM + ML_ROPE_DIM
        q_scale = qd ** -0.5 * LOG2_E
        qt, k, vt = _mla_b_call(qkv, wts["wq"], wts["wkv"], tables_t(ML_ROPE_DIM, None, q_scale),
                                seq=seq, ctx_len=ctx_len, q_scale=q_scale)
        a3 = _attn_call("mla", qt, b3(k), vt, b3(g), [], seq=seq, need_ctx=need_ctx)
    else:
        raise ValueError(kind)
    return _finish_call(a3, h3, mod3, p["post_g"], wts["w_out"], seq=seq, nxt=nxt)


def kernel(x, c, ctx, c_ctx, l0_ada_w, l0_ada_b, l0_pre_g, l0_post_g, l0_w_in, l0_lam_q1, l0_lam_k1, l0_lam_q2, l0_lam_k2, l0_subln_g, l0_w_out, l1_ada_w, l1_ada_b, l1_pre_g, l1_post_g, l1_w_in, l1_q_norm_g, l1_k_norm_g, l1_w_out, l2_ada_w, l2_ada_b, l2_pre_g, l2_post_g, l2_w_in, l2_q_a_norm_g, l2_w_q_b, l2_kv_a_norm_g, l2_w_kv_b, l2_w_out, l3_ada_w, l3_ada_b, l3_pre_g, l3_post_g, l3_w_in, l3_lam_q1, l3_lam_k1, l3_lam_q2, l3_lam_k2, l3_subln_g, l3_w_out):
    batch, seq, d = x.shape
    ctx_len = ctx.shape[1]
    assert d == D_MODEL and seq % GRID_W == 0 and batch + 1 <= MOD_ROWS
    diff_names = ("ada_w", "ada_b", "pre_g", "post_g", "w_in", "lam_q1", "lam_k1", "lam_q2",
                  "lam_k2", "subln_g", "w_out")
    layers = [
        ("diff", dict(zip(diff_names, (l0_ada_w, l0_ada_b, l0_pre_g, l0_post_g, l0_w_in,
                                       l0_lam_q1, l0_lam_k1, l0_lam_q2, l0_lam_k2, l0_subln_g,
                                       l0_w_out)))),
        ("gqa", dict(ada_w=l1_ada_w, ada_b=l1_ada_b, pre_g=l1_pre_g, post_g=l1_post_g,
                     w_in=l1_w_in, q_norm_g=l1_q_norm_g, k_norm_g=l1_k_norm_g, w_out=l1_w_out)),
        ("mla", dict(ada_w=l2_ada_w, ada_b=l2_ada_b, pre_g=l2_pre_g, post_g=l2_post_g,
                     w_in=l2_w_in, q_a_norm_g=l2_q_a_norm_g, w_q_b=l2_w_q_b,
                     kv_a_norm_g=l2_kv_a_norm_g, w_kv_b=l2_w_kv_b, w_out=l2_w_out)),
        ("diff", dict(zip(diff_names, (l3_ada_w, l3_ada_b, l3_pre_g, l3_post_g, l3_w_in,
                                       l3_lam_q1, l3_lam_k1, l3_lam_q2, l3_lam_k2, l3_subln_g,
                                       l3_w_out)))),
    ]
    cc = jnp.zeros((MOD_ROWS, d), F32).at[:batch].set(c).at[batch].set(c_ctx)
    h3 = (x, ctx)
    kind0, p0 = layers[0]
    mod = _ada_call(cc, p0["ada_w"], p0["ada_b"])
    to_cast, ready = _prep_weights(kind0, p0)
    wts = dict(ready, **{name: w.astype(BF16) for name, w in to_cast.items()})
    for layer, (kind, p) in enumerate(layers):
        nxt, ready = None, {}
        if layer + 1 < len(layers):
            kind_n, p_n = layers[layer + 1]
            to_cast, ready = _prep_weights(kind_n, p_n)
            nxt = (cc, p_n["ada_w"], p_n["ada_b"], to_cast)
        h3, mod, cast = _layer(kind, layer, h3, mod, wts, p, nxt, seq=seq, ctx_len=ctx_len)
        wts = dict(ready, **(cast or {}))
    return h3
```

```python
import functools
import math

import jax
import jax.numpy as jnp
import numpy as np
from jax import lax
from jax.experimental import pallas as pl
from jax.experimental.pallas import tpu as pltpu

F32 = jnp.float32
BF16 = jnp.bfloat16
LOG2_E = math.log2(math.e)

D_MODEL = 2048
GRID_W = 64
ROPE_BASE = 10000.0
NORM_EPS = 1e-6
DA_HEAD_DIM = 64
DA_HEADS = D_MODEL // (2 * DA_HEAD_DIM)
GQ_HEAD_DIM = 128
GQ_HEADS = D_MODEL // GQ_HEAD_DIM
GQ_KV_HEADS = GQ_HEADS // 4
GQ_KV_W = GQ_KV_HEADS * GQ_HEAD_DIM
ML_NOPE_DIM = 128
ML_ROPE_DIM = 64
ML_V_DIM = 128
ML_HEADS = D_MODEL // ML_V_DIM
ML_Q_RANK = D_MODEL // 4
ML_KV_RANK = D_MODEL // 4

LANES = 128
SUBLANES = 8
MXU_N = 512
ROW_CHUNK = 128
VMEM_LIMIT = 56 * 1024 * 1024
MOD_ROWS = 16

PROJ_TM = 768
FINISH_TM = 384
FINISH_TM_LAST = 512
PROJ_SUB = 256
FINISH_SUB = 128
MLA_HEADS_PER_STEP = 4
DA_HEADS_PER_STEP = 2
KEY_CHUNK = 256


def _silu(x):
    return x / (1.0 + jnp.exp(-x))


def _rms(x, eps=NORM_EPS):
    return x * lax.rsqrt(jnp.mean(x * x, axis=-1, keepdims=True) + eps)


def _rope(x, cos, sin_a, sin_b, half):
    return (x * cos + pltpu.roll(x, LANES - half, 1) * sin_a + pltpu.roll(x, half, 1) * sin_b)


def _swap_halves(x, half):
    parts = []
    for r in range(0, x.shape[0], 2 * half):
        parts += [x[r + half:r + 2 * half], x[r:r + half]]
    return jnp.concatenate(parts, axis=0)


def _rope_t(x, cos_t, sin_t, half):
    return x * cos_t + _swap_halves(x, half) * sin_t


def _rope_tables(seq, ctx_len, head_dim):
    t = seq + ctx_len
    r = np.arange(t)
    lat = (r < seq)[:, None]
    rowpos = (r // GRID_W).astype(np.float32)[:, None]
    colpos = (r % GRID_W).astype(np.float32)[:, None]
    lane = np.arange(LANES)
    u = lane % head_dim
    half2 = head_dim // 2
    half = half2 // 2
    grp = (u // half2)[None, :]
    w = u % half2
    first = (w < half)[None, :]
    inv_freq = (np.float32(ROPE_BASE) ** (-(w % half).astype(np.float32) / np.float32(half)))[None, :]
    ang = (np.where(grp == 0, rowpos, colpos) * inv_freq).astype(np.float32)
    cos = np.where(lat, np.cos(ang), 1.0).astype(np.float32)
    sin = np.where(lat, np.sin(ang), 0.0).astype(np.float32)
    sin_a = np.where(first, -sin, np.float32(0.0))
    sin_b = np.where(first, np.float32(0.0), sin)
    return cos, sin_a, sin_b


def _rope_tables_t(seq, ctx_len, head_dim, gain, scale):
    cos, sin_a, sin_b = _rope_tables(seq, ctx_len, head_dim)
    half = head_dim // 4
    lane = np.arange(LANES)
    partner = np.where((lane % (2 * half)) < half, lane + half, lane - half)
    cos_t = jnp.asarray(np.ascontiguousarray((cos * np.float32(scale)).T))
    sin_t = jnp.asarray(np.ascontiguousarray(((sin_a + sin_b) * np.float32(scale)).T))
    if gain is None:
        return cos_t, sin_t
    g = jnp.tile(gain.astype(F32), LANES // head_dim)
    return cos_t * g[:, None], sin_t * g[partner][:, None]


def _ada_kernel(c_ref, w_ref, b_ref, o_ref):
    s = _silu(c_ref[...])
    o_ref[...] = jnp.dot(s.astype(BF16), w_ref[...].astype(BF16),
                         preferred_element_type=F32) + b_ref[...]


def _ada_call(cc, ada_w, ada_b):
    d, n = ada_w.shape
    tn = 768
    return pl.pallas_call(
        _ada_kernel,
        grid=(n // tn,),
        in_specs=[pl.BlockSpec((MOD_ROWS, d), lambda j: (0, 0)),
                  pl.BlockSpec((d, tn), lambda j: (0, j)),
                  pl.BlockSpec((1, tn), lambda j: (0, j))],
        out_specs=pl.BlockSpec((MOD_ROWS, tn), lambda j: (0, j)),
        out_shape=jax.ShapeDtypeStruct((MOD_ROWS, n), F32),
        compiler_params=pltpu.CompilerParams(dimension_semantics=("arbitrary",),
                                             vmem_limit_bytes=VMEM_LIMIT),
        name="ada",
    )(cc, ada_w, ada_b.reshape(1, n))


_KIND_OUTS = {"da_q": ("qt",), "gq_q": ("qt",), "da_k": ("o",), "gq_kv": ("o", "vt"),
              "v": ("vt",), "g": ("g",), "ml_qa": ("o",), "ml_kva": ("o",), "ml_kr": ("o",)}


def _mod_rows(mod_b_ref, mod_c_ref, is_ctx, part):
    d = D_MODEL
    mb = mod_b_ref[:, part * d:(part + 1) * d]
    mc = mod_c_ref[:, part * d:(part + 1) * d]
    return jnp.where(is_ctx, mc, mb)


def _h_chunk(h_refs, n_ctx_refs, c, is_ctx):
    if len(h_refs) == 1:
        return h_refs[0][c * ROW_CHUNK:(c + 1) * ROW_CHUNK, :]
    n_sb = len(h_refs) - n_ctx_refs
    per_sb = h_refs[0].shape[0] // ROW_CHUNK
    r, rows = c // per_sb, slice((c % per_sb) * ROW_CHUNK, (c % per_sb + 1) * ROW_CHUNK)
    v = h_refs[r][rows, :]
    if r >= n_sb - n_ctx_refs:
        v = jnp.where(is_ctx, h_refs[n_sb + r - (n_sb - n_ctx_refs)][rows, :], v)
    return v


def _split_h_specs(x, ctx, tm, sub, idx):
    batch, seq, d = x.shape
    ctx_len = ctx.shape[1]
    assert tm % sub == 0 and seq % sub == 0 and ctx_len % sub == 0 and (seq + ctx_len) % tm == 0
    n_sb, last = tm // sub, seq // sub - 1
    xv = x.reshape(batch, seq // sub, sub, d)
    cv = ctx.reshape(batch, ctx_len // sub, sub, d)
    specs = [pl.BlockSpec((None, None, sub, d),
                          lambda *g, r=r: (idx(*g)[0], jnp.minimum(idx(*g)[1] * n_sb + r, last), 0, 0))
             for r in range(n_sb)]
    specs += [pl.BlockSpec((None, None, sub, d), lambda *g, q=q: (idx(*g)[0], q, 0, 0))
              for q in range(ctx_len // sub)]
    return [xv] * n_sb + [cv] * (ctx_len // sub), specs, ctx_len // sub


def _prologue(h_refs, n_ctx_refs, mod_b_ref, mod_c_ref, preg_ref, xn_ref, slot, row0, seq, pieces):
    preg = preg_ref[...]
    for c in pieces:
        rows = pl.ds(c * ROW_CHUNK, ROW_CHUNK)
        is_ctx = (row0 + c * ROW_CHUNK) >= seq
        shift = _mod_rows(mod_b_ref, mod_c_ref, is_ctx, 0)
        scale = _mod_rows(mod_b_ref, mod_c_ref, is_ctx, 1)
        y = _rms(_h_chunk(h_refs, n_ctx_refs, c, is_ctx)) * preg
        xn_ref[slot, rows, :] = (y * (1.0 + scale) + shift).astype(BF16)


def _proj_kernel(*refs, kinds, out_names, n_h, n_ctx_refs, n_tabs, seq, n_tiles, tiles_per_batch,
                 rope_half):
    h_refs, refs = refs[:n_h], refs[n_h:]
    mod_b_ref, mod_c_ref, preg_ref, w_ref = refs[:4]
    tabs = refs[4:4 + n_tabs]
    ng0_ref, ng1_ref = refs[4 + n_tabs:6 + n_tabs]
    outs = dict(zip(out_names, refs[6 + n_tabs:]))
    xn_ref = refs[-1]
    i = pl.program_id(0)
    j = pl.program_id(1)
    tm = xn_ref.shape[1]
    n_steps = len(kinds)
    n_pieces = tm // ROW_CHUNK
    row0 = (jnp.minimum(i, n_tiles - 1) % tiles_per_batch) * tm
    prologue = functools.partial(_prologue, h_refs, n_ctx_refs, mod_b_ref, mod_c_ref, preg_ref,
                                 xn_ref, i % 2, row0, seq)

    @pl.when((i == 0) & (j == 0))
    def _():
        prologue(range(n_pieces))

    def heads_t(a):
        at = a.T
        return [at[c * LANES:(c + 1) * LANES] for c in range(at.shape[0] // LANES)]

    def head_rms(xh):
        return lax.rsqrt(jnp.mean(xh * xh, axis=0, keepdims=True) + NORM_EPS)

    def tile(kind):
        a = jnp.dot(xn_ref[(i + 1) % 2], w_ref[...], preferred_element_type=F32)
        if kind == "g":
            outs["g"][...] = a
        elif kind == "v":
            outs["vt"][...] = a.T.astype(BF16)
        elif kind in ("da_q", "gq_q"):
            cos_t, sin_t = tabs[0][...], tabs[1][...]
            for c, xh in enumerate(heads_t(a)):
                y = _rope_t(xh, cos_t, sin_t, rope_half)
                if kind == "gq_q":
                    y = y * head_rms(xh)
                outs["qt"][c * LANES:(c + 1) * LANES, :] = y.astype(BF16)
        elif kind in ("da_k", "gq_kv"):
            cos_t, sin_t = tabs[2][...], tabs[3][...]
            k_cols = outs["o"].shape[1]
            for c, xh in enumerate(heads_t(a[:, :k_cols])):
                y = _rope_t(xh, cos_t, sin_t, rope_half)
                if kind == "gq_kv":
                    y = y * head_rms(xh)
                outs["o"][:, c * LANES:(c + 1) * LANES] = y.T.astype(BF16)
            if kind == "gq_kv":
                outs["vt"][...] = a[:, k_cols:].T.astype(BF16)
        elif kind == "ml_qa":
            outs["o"][...] = (_rms(a) * ng0_ref[...]).astype(BF16)
        elif kind == "ml_kva":
            outs["o"][...] = (_rms(a) * ng1_ref[...]).astype(BF16)
        elif kind == "ml_kr":
            outs["o"][...] = a.astype(BF16)
            outs["o"][:, :LANES] = _rope(a[:, :LANES], tabs[0][...], tabs[1][...], tabs[2][...],
                                         rope_half).astype(BF16)
        else:
            raise ValueError(kind)

    def step(jv):
        tile(kinds[jv])
        prologue([p for p in range(n_pieces) if n_steps - 1 - p % n_steps == jv])

    for jv in range(n_steps):
        pl.when((i > 0) & (j == jv))(functools.partial(step, jv))


def _proj_call(h, mod3, pre_g, w, tabs, ng0, ng1, *, kinds, tn, o_cols, v_rows, tabs_transposed,
               batch, seq, ctx_len, rope_half):
    d = D_MODEL
    m = batch * (seq + ctx_len)
    n = w.shape[1]
    t = seq + ctx_len
    tm = PROJ_TM
    tpb = t // tm
    n_tiles = m // tm
    assert t % tm == 0 and n % tn == 0 and len(kinds) == n // tn

    def nxt(i):
        return jnp.minimum(i, n_tiles - 1)

    def cur(i):
        return jnp.maximum(i - 1, 0)

    def tile_idx(name):
        js = [jv for jv, k in enumerate(kinds) if name in _KIND_OUTS[k]]
        assert js == list(range(js[0], js[-1] + 1))
        return lambda i, j: jnp.where(i == 0, 0, jnp.clip(j - js[0], 0, len(js) - 1)), len(js)

    out_names, out_specs, out_shape = [], [], []
    for name in ("qt", "o", "vt", "g"):
        if not any(name in _KIND_OUTS[k] for k in kinds):
            continue
        idx, cnt = tile_idx(name)
        out_names.append(name)
        if name in ("qt", "vt"):
            rows = tn if name == "qt" else v_rows
            out_specs.append(pl.BlockSpec((rows, tm), lambda i, j, idx=idx: (idx(i, j), cur(i))))
            out_shape.append(jax.ShapeDtypeStruct((cnt * rows, m), BF16))
        else:
            cols, dt = (o_cols, BF16) if name == "o" else (tn, F32)
            out_specs.append(pl.BlockSpec((tm, cols), lambda i, j, idx=idx: (cur(i), idx(i, j))))
            out_shape.append(jax.ShapeDtypeStruct((m, cnt * cols), dt))
    assert out_shape[-1].shape == (m, d)

    if tabs_transposed:
        tab_spec = pl.BlockSpec((LANES, tm), lambda i, j: (0, cur(i) % tpb))
    else:
        tab_spec = pl.BlockSpec((tm, LANES), lambda i, j: (cur(i) % tpb, 0))
    if isinstance(h, tuple):
        h_args, h_specs, n_ctx_refs = _split_h_specs(
            *h, tm, PROJ_SUB, lambda i, j: (nxt(i) // tpb, nxt(i) % tpb))
    else:
        h_args, h_specs, n_ctx_refs = [h], [pl.BlockSpec((tm, d), lambda i, j: (nxt(i), 0))], 0
    kern = functools.partial(_proj_kernel, kinds=tuple(kinds), out_names=tuple(out_names),
                             n_h=len(h_args), n_ctx_refs=n_ctx_refs, n_tabs=len(tabs), seq=seq,
                             n_tiles=n_tiles, tiles_per_batch=tpb, rope_half=rope_half)
    return pl.pallas_call(
        kern,
        grid=(n_tiles + 1, n // tn),
        in_specs=h_specs + [
            pl.BlockSpec((None, 1, 3 * d), lambda i, j: (nxt(i) // tpb, 0, 0)),
            pl.BlockSpec((None, 1, 3 * d), lambda i, j: (batch, 0, 0)),
            pl.BlockSpec((1, d), lambda i, j: (0, 0)),
            pl.BlockSpec((d, tn), lambda i, j: (0, jnp.where(i == 0, 0, j))),
        ] + [tab_spec] * len(tabs) + [
            pl.BlockSpec(ng0.shape, lambda i, j: (0, 0)),
            pl.BlockSpec(ng1.shape, lambda i, j: (0, 0)),
        ],
        out_specs=out_specs,
        out_shape=out_shape,
        scratch_shapes=[pltpu.VMEM((2, tm, d), BF16)],
        compiler_params=pltpu.CompilerParams(dimension_semantics=("arbitrary", "arbitrary"),
                                             vmem_limit_bytes=VMEM_LIMIT),
        name="proj",
    )(*h_args, mod3, mod3, pre_g.reshape(1, d), w, *tabs, ng0, ng1)


def _mla_b_kernel(qa_ref, kva_ref, kr_ref, wq_ref, wkn_ref, wv_ref, cos_t_ref, sin_t_ref,
                  qt_ref, k_ref, vt_ref, *, q_scale):
    qt = jnp.dot(qa_ref[...], wq_ref[...], preferred_element_type=F32).T
    for c in range(qt.shape[0] // LANES):
        x = qt[c * LANES:(c + 1) * LANES]
        if c % 2 == 1:
            x = _rope_t(x, cos_t_ref[...], sin_t_ref[...], ML_ROPE_DIM // 4)
        else:
            x = x * q_scale
        qt_ref[c * LANES:(c + 1) * LANES, :] = x.astype(BF16)
    kva = kva_ref[...]
    kn = jnp.dot(kva, wkn_ref[...], preferred_element_type=F32)
    kr = kr_ref[...]
    for hh in range(kn.shape[1] // LANES):
        k_ref[:, 2 * hh * LANES:(2 * hh + 1) * LANES] = kn[:, hh * LANES:(hh + 1) * LANES].astype(BF16)
        k_ref[:, (2 * hh + 1) * LANES:(2 * hh + 2) * LANES] = kr
    vt_ref[...] = jnp.dot(kva, wv_ref[...], preferred_element_type=F32).T.astype(BF16)


def _mla_b_call(qkv, wq, wkv, tabs_t, *, seq, ctx_len, q_scale):
    m = qkv.shape[0]
    tm = PROJ_TM
    tpb = (seq + ctx_len) // tm
    r = ML_Q_RANK
    nq = wq.shape[1]
    nkv = wkv.shape[1] // 2
    steps = 4
    tq, tk = nq // steps, nkv // steps
    kr_col = (ML_Q_RANK + ML_KV_RANK) // LANES
    tab_spec = pl.BlockSpec((LANES, tm), lambda i, j: (0, i % tpb))
    return pl.pallas_call(
        functools.partial(_mla_b_kernel, q_scale=q_scale),
        grid=(m // tm, steps),
        in_specs=[
            pl.BlockSpec((tm, r), lambda i, j: (i, 0)),
            pl.BlockSpec((tm, r), lambda i, j: (i, 1)),
            pl.BlockSpec((tm, LANES), lambda i, j: (i, kr_col)),
            pl.BlockSpec((r, tq), lambda i, j: (0, j)),
            pl.BlockSpec((r, tk), lambda i, j: (0, j)),
            pl.BlockSpec((r, tk), lambda i, j: (0, steps + j)),
            tab_spec, tab_spec,
        ],
        out_specs=[
            pl.BlockSpec((tq, tm), lambda i, j: (j, i)),
            pl.BlockSpec((tm, 2 * tk), lambda i, j: (i, j)),
            pl.BlockSpec((tk, tm), lambda i, j: (j, i)),
        ],
        out_shape=[jax.ShapeDtypeStruct((nq, m), BF16),
                   jax.ShapeDtypeStruct((m, 2 * nkv), BF16),
                   jax.ShapeDtypeStruct((nkv, m), BF16)],
        compiler_params=pltpu.CompilerParams(dimension_semantics=("arbitrary", "arbitrary"),
                                             vmem_limit_bytes=VMEM_LIMIT),
        name="mla_b",
    )(qkv, qkv, qkv, wq, wkv, wkv, *tabs_t)


def _sublane_groups(x):
    return x.reshape(x.shape[0] // SUBLANES, SUBLANES, x.shape[1])


def _run_chains(chains, s_scr, e_scr):
    n = len(chains)
    shape, m, l = {}, {}, {}
    for step in range(n + 2):
        if step < n:
            qt, k, vt, _ = chains[step]
            qv = qt()
            nq, nk = qv.shape[1], vt().shape[1]
            shape[step] = (nk, nq)
            kc = min(KEY_CHUNK, nk)
            m8 = None
            for c0 in range(0, nk, kc):
                st = jnp.dot(k(c0, kc), qv, preferred_element_type=F32)
                s_scr[step % 2, c0:c0 + kc, :nq] = st
                cm = jnp.max(_sublane_groups(st), axis=0)
                m8 = cm if m8 is None else jnp.maximum(m8, cm)
            m[step] = jnp.max(m8, axis=0, keepdims=True)
        i = step - 1
        if 0 <= i < n:
            nk, nq = shape[i]
            mi = m.pop(i)
            kc = min(KEY_CHUNK, nk)
            l8 = None
            for c0 in range(0, nk, kc):
                p = jnp.exp2(s_scr[i % 2, c0:c0 + kc, :nq] - mi)
                ps = jnp.sum(_sublane_groups(p), axis=0)
                l8 = ps if l8 is None else l8 + ps
                e_scr[i % 2, c0:c0 + kc, :nq] = p.astype(BF16)
            l[i] = jnp.sum(l8, axis=0, keepdims=True)
        i = step - 2
        if 0 <= i < n:
            nk, nq = shape[i]
            _, _, vt, done = chains[i]
            done(jnp.dot(vt(), e_scr[i % 2, :nk, :nq], preferred_element_type=F32) / l.pop(i))


def _row_blocks(rows_q, seq):
    assert seq % MXU_N == 0
    blocks = [(r0, MXU_N, 0) for r0 in range(0, seq, MXU_N)]
    if rows_q > seq:
        blocks.append((seq, rows_q - seq, seq))
    return blocks


def _diff_attn_kernel(lq1_ref, lk1_ref, lq2_ref, lk2_ref, subg_ref, qt_ref, k_ref, vt_ref, g_ref,
                      o_ref, s_scr, e_scr, *, seq, lambda_init):
    lam = (jnp.exp(jnp.sum(lq1_ref[...] * lk1_ref[...], axis=-1, keepdims=True))
           - jnp.exp(jnp.sum(lq2_ref[...] * lk2_ref[...], axis=-1, keepdims=True))
           + lambda_init)
    chains = []
    for r0, nr, key0 in _row_blocks(o_ref.shape[0], seq):
        rows = slice(r0, r0 + nr)
        for hh in range(qt_ref.shape[0] // LANES):
            cols = slice(hh * LANES, (hh + 1) * LANES)
            parts = []

            def q_sub(sub, rows=rows, cols=cols):
                qf = qt_ref[cols, rows].astype(F32)
                row = lax.broadcasted_iota(jnp.int32, qf.shape, 0)
                return jnp.where((row >= DA_HEAD_DIM) == bool(sub), qf, 0.0).astype(BF16)

            def done(ot, rows=rows, cols=cols, parts=parts):
                parts.append(ot)
                if len(parts) == 2:
                    o = (parts[0] - lam * parts[1]).T
                    o = _rms(o) * subg_ref[...] * (1.0 - lambda_init)
                    o_ref[rows, cols] = (o * _silu(g_ref[rows, cols])).astype(BF16)

            for sub in range(2):
                chains.append((functools.partial(q_sub, sub),
                               lambda c0, kc, key0=key0, cols=cols: k_ref[key0 + c0:key0 + c0 + kc,
                                                                          cols],
                               lambda key0=key0, cols=cols: vt_ref[cols, key0:], done))
    _run_chains(chains, s_scr, e_scr)


def _plain_attn_kernel(qt_ref, k_ref, vt_ref, g_ref, o_ref, s_scr, e_scr, *, seq, dq, shared_kv):
    n_heads = qt_ref.shape[0] // dq
    chains = []
    for r0, nr, key0 in _row_blocks(o_ref.shape[0], seq):
        rows = slice(r0, r0 + nr)
        for hh in range(n_heads):
            kh = 0 if shared_kv else hh
            cols = slice(hh * LANES, (hh + 1) * LANES)

            def done(ot, rows=rows, cols=cols):
                o_ref[rows, cols] = (ot.T * _silu(g_ref[rows, cols])).astype(BF16)

            chains.append((lambda rows=rows, hh=hh: qt_ref[hh * dq:(hh + 1) * dq, rows],
                           lambda c0, kc, key0=key0, kh=kh: k_ref[key0 + c0:key0 + c0 + kc,
                                                                  kh * dq:(kh + 1) * dq],
                           lambda key0=key0, kh=kh: vt_ref[kh * LANES:(kh + 1) * LANES, key0:],
                           done))
    _run_chains(chains, s_scr, e_scr)


def _attn_call(kind, qt, k3, vt, g3, small, *, seq, need_ctx, lambda_init=None):
    batch, t, _ = k3.shape
    rows = t if need_ctx else seq
    heads = D_MODEL // LANES
    if kind == "diff":
        hps, kv_heads, dq, k_col0 = DA_HEADS_PER_STEP, DA_HEADS_PER_STEP, LANES, 0
        kern = functools.partial(_diff_attn_kernel, seq=seq, lambda_init=lambda_init)
    elif kind == "gqa":
        hps, kv_heads, dq, k_col0 = GQ_HEADS // GQ_KV_HEADS, 1, LANES, 0
        kern = functools.partial(_plain_attn_kernel, seq=seq, dq=dq, shared_kv=True)
    elif kind == "mla":
        hps, kv_heads, dq, k_col0 = MLA_HEADS_PER_STEP, MLA_HEADS_PER_STEP, 2 * LANES, 0
        kern = functools.partial(_plain_attn_kernel, seq=seq, dq=dq, shared_kv=False)
    else:
        raise ValueError(kind)
    kw = kv_heads * dq
    in_specs = [pl.BlockSpec(a.shape, lambda b, s: (0,) * a.ndim) for a in small] + [
        pl.BlockSpec((hps * dq, t), lambda b, s: (s, b)),
        pl.BlockSpec((None, t, kw), lambda b, s: (b, 0, k_col0 + s)),
        pl.BlockSpec((kv_heads * LANES, t), lambda b, s: (s, b)),
        pl.BlockSpec((None, rows, hps * LANES), lambda b, s: (b, 0, s)),
    ]
    return pl.pallas_call(
        kern,
        grid=(batch, heads // hps),
        in_specs=in_specs,
        out_specs=pl.BlockSpec((None, rows, hps * LANES), lambda b, s: (b, 0, s)),
        out_shape=jax.ShapeDtypeStruct((batch, rows, D_MODEL), BF16),
        scratch_shapes=[pltpu.VMEM((2, t, MXU_N), F32), pltpu.VMEM((2, t, MXU_N), BF16)],
        compiler_params=pltpu.CompilerParams(dimension_semantics=("arbitrary", "arbitrary"),
                                             vmem_limit_bytes=VMEM_LIMIT),
        name="attn_" + kind,
    )(*small, qt, k3, vt, g3)


def _finish_kernel(*refs, seq, n_h, n_ctx_refs, n_cast):
    h_refs, refs = refs[:n_h], refs[n_h:]
    a_ref, mod_b_ref, mod_c_ref, postg_ref, w_ref = refs[:5]
    n_in = 5 + (3 + n_cast if n_cast else 0)
    o_ref = refs[n_in]
    ti = pl.program_id(1)
    tm = a_ref.shape[0]
    y = jnp.dot(a_ref[...], w_ref[...], preferred_element_type=F32)
    postg = postg_ref[...]
    for c in range(tm // ROW_CHUNK):
        rows = slice(c * ROW_CHUNK, (c + 1) * ROW_CHUNK)
        is_ctx = (ti * tm + c * ROW_CHUNK) >= seq
        gate = _mod_rows(mod_b_ref, mod_c_ref, is_ctx, 2)
        o_ref[rows, :] = (_h_chunk(h_refs, n_ctx_refs, c, is_ctx)
                          + gate * (_rms(y[rows, :]) * postg))
    if n_cast:
        c_ref, adaw_ref, adab_ref = refs[5:8]
        _ada_kernel(c_ref, adaw_ref, adab_ref, refs[n_in + 1])
        for src, dst in zip(refs[8:n_in], refs[n_in + 2:]):
            dst[...] = src[...].astype(BF16)


def _finish_call(a3, h, mod3, post_g, w, *, seq, nxt=None):
    batch, rows_out, d = a3.shape
    t = sum(v.shape[1] for v in h) if isinstance(h, tuple) else h.shape[1]
    tm = FINISH_TM if rows_out == t else FINISH_TM_LAST
    assert rows_out % tm == 0
    n_i = rows_out // tm
    blk = pl.BlockSpec((None, tm, d), lambda b, i: (b, i, 0))
    if isinstance(h, tuple):
        h_args, h_specs, n_ctx_refs = _split_h_specs(*h, tm, FINISH_SUB, lambda b, i: (b, i))
    else:
        h_args, h_specs, n_ctx_refs = [h], [blk], 0
    in_specs = h_specs + [
        blk,
        pl.BlockSpec((None, 1, 3 * d), lambda b, i: (b, 0, 0)),
        pl.BlockSpec((None, 1, 3 * d), lambda b, i: (batch, 0, 0)),
        pl.BlockSpec((1, d), lambda b, i: (0, 0)),
        pl.BlockSpec((d, d), lambda b, i: (0, 0)),
    ]
    args = h_args + [a3, mod3, mod3, post_g.reshape(1, d), w]
    out_specs = [blk]
    out_shape = [jax.ShapeDtypeStruct((batch, rows_out, d), F32)]
    names = []
    if nxt is not None:
        cc, ada_w, ada_b, weights = nxt
        n_mod = ada_w.shape[1]
        assert n_mod == batch * n_i * LANES
        flat = lambda b, i: (0, b * n_i + i)
        in_specs += [pl.BlockSpec((MOD_ROWS, d), lambda b, i: (0, 0)),
                     pl.BlockSpec((d, LANES), flat), pl.BlockSpec((1, LANES), flat)]
        args += [cc, ada_w, ada_b.reshape(1, n_mod)]
        out_specs.append(pl.BlockSpec((MOD_ROWS, LANES), flat))
        out_shape.append(jax.ShapeDtypeStruct((MOD_ROWS, n_mod), F32))
        for name, wf in weights.items():
            r, c = wf.shape
            assert r % (batch * 2 * SUBLANES) == 0 and c % (4 * LANES) == 0 and n_i >= 4
            spec = pl.BlockSpec((r // batch, c // 4), lambda b, i: (b, jnp.minimum(i, 3)))
            names.append(name)
            in_specs.append(spec)
            args.append(wf)
            out_specs.append(spec)
            out_shape.append(jax.ShapeDtypeStruct((r, c), BF16))
    outs = pl.pallas_call(
        functools.partial(_finish_kernel, seq=seq, n_h=len(h_args), n_ctx_refs=n_ctx_refs,
                          n_cast=len(names)),
        grid=(batch, n_i),
        in_specs=in_specs,
        out_specs=out_specs,
        out_shape=out_shape,
        compiler_params=pltpu.CompilerParams(dimension_semantics=("arbitrary", "arbitrary"),
                                             vmem_limit_bytes=VMEM_LIMIT),
        name="finish",
    )(*args)
    if nxt is None:
        return outs[0], None, None
    return outs[0], outs[1], dict(zip(names, outs[2:]))


def _lambda_init(layer):
    return 0.8 - 0.6 * math.exp(-0.3 * layer)


def _prep_weights(kind, p):
    if kind != "mla":
        return dict(w_in=p["w_in"], w_out=p["w_out"]), {}
    w_in = p["w_in"].astype(BF16)
    d = w_in.shape[0]
    r2 = ML_Q_RANK + ML_KV_RANK
    ready = {}
    ready["w_in"] = jnp.concatenate(
        [w_in[:, :r2 + ML_ROPE_DIM], jnp.zeros((d, ML_Q_RANK - ML_ROPE_DIM), BF16),
         w_in[:, r2 + ML_ROPE_DIM:]], axis=1)
    qd = ML_NOPE_DIM + ML_ROPE_DIM
    wq = p["w_q_b"].astype(BF16).reshape(ML_Q_RANK, ML_HEADS, qd)
    wq = jnp.pad(wq, ((0, 0), (0, 0), (0, 2 * LANES - qd)))
    ready["wq"] = wq.reshape(ML_Q_RANK, ML_HEADS * 2 * LANES)
    wkv = p["w_kv_b"].astype(BF16).reshape(ML_KV_RANK, ML_HEADS, ML_NOPE_DIM + ML_V_DIM)
    ready["wkv"] = jnp.concatenate([wkv[:, :, :ML_NOPE_DIM].reshape(ML_KV_RANK, -1),
                                    wkv[:, :, ML_NOPE_DIM:].reshape(ML_KV_RANK, -1)], axis=1)
    return dict(w_out=p["w_out"]), ready


def _layer(kind, layer, h3, mod, wts, p, nxt, *, seq, ctx_len):
    d = D_MODEL
    t = seq + ctx_len
    batch = h3[0].shape[0] if isinstance(h3, tuple) else h3.shape[0]
    need_ctx = nxt is not None
    mod3 = mod.reshape(MOD_ROWS, 1, 3 * d)
    h2 = h3 if isinstance(h3, tuple) else h3.reshape(batch * t, d)
    one = jnp.ones((1, LANES), F32)
    common = dict(batch=batch, seq=seq, ctx_len=ctx_len)
    tables_t = functools.partial(_rope_tables_t, seq, ctx_len)

    def b3(a):
        return a.reshape(batch, t, a.shape[-1])

    if kind == "diff":
        tabs = (tables_t(DA_HEAD_DIM, None, DA_HEAD_DIM ** -0.5 * LOG2_E)
                + tables_t(DA_HEAD_DIM, None, 1.0))
        qt, k, vt, g = _proj_call(h2, mod3, p["pre_g"], wts["w_in"], tabs, one, one,
                                  kinds=["da_q"] * 2 + ["da_k"] * 2 + ["v"] * 2 + ["g"] * 2,
                                  tn=1024, o_cols=1024, v_rows=1024, tabs_transposed=True,
                                  rope_half=DA_HEAD_DIM // 4, **common)
        small = [p[n].reshape(1, DA_HEAD_DIM) for n in ("lam_q1", "lam_k1", "lam_q2", "lam_k2")]
        small.append(p["subln_g"].reshape(1, 2 * DA_HEAD_DIM))
        a3 = _attn_call("diff", qt, b3(k), vt, b3(g), small, seq=seq, need_ctx=need_ctx,
                        lambda_init=_lambda_init(layer))
    elif kind == "gqa":
        tabs = (tables_t(GQ_HEAD_DIM, p["q_norm_g"], GQ_HEAD_DIM ** -0.5 * LOG2_E)
                + tables_t(GQ_HEAD_DIM, p["k_norm_g"], 1.0))
        qt, k, vt, g = _proj_call(h2, mod3, p["pre_g"], wts["w_in"], tabs, one, one,
                                  kinds=["gq_q"] * 2 + ["gq_kv"] + ["g"] * 2,
                                  tn=1024, o_cols=GQ_KV_W, v_rows=GQ_KV_W, tabs_transposed=True,
                                  rope_half=GQ_HEAD_DIM // 4, **common)
        a3 = _attn_call("gqa", qt, b3(k), vt, b3(g), [], seq=seq, need_ctx=need_ctx)
    elif kind == "mla":
        tn = ML_Q_RANK
        qkv, g = _proj_call(h2, mod3, p["pre_g"], wts["w_in"],
                            _rope_tables(seq, ctx_len, ML_ROPE_DIM),
                            p["q_a_norm_g"].reshape(1, ML_Q_RANK),
                            p["kv_a_norm_g"].reshape(1, ML_KV_RANK),
                            kinds=["ml_qa", "ml_kva", "ml_kr"] + ["g"] * (d // tn),
                            tn=tn, o_cols=tn, v_rows=0, tabs_transposed=False,
                            rope_half=ML_ROPE_DIM // 4, **common)
        qd = ML_NOPE_DIM + ML_ROPE_DIM
        q_scale = qd ** -0.5 * LOG2_E
        qt, k, vt = _mla_b_call(qkv, wts["wq"], wts["wkv"], tables_t(ML_ROPE_DIM, None, q_scale),
                                seq=seq, ctx_len=ctx_len, q_scale=q_scale)
        a3 = _attn_call("mla", qt, b3(k), vt, b3(g), [], seq=seq, need_ctx=need_ctx)
    else:
        raise ValueError(kind)
    return _finish_call(a3, h3, mod3, p["post_g"], wts["w_out"], seq=seq, nxt=nxt)


def kernel(x, c, ctx, c_ctx, l0_ada_w, l0_ada_b, l0_pre_g, l0_post_g, l0_w_in, l0_lam_q1, l0_lam_k1, l0_lam_q2, l0_lam_k2, l0_subln_g, l0_w_out, l1_ada_w, l1_ada_b, l1_pre_g, l1_post_g, l1_w_in, l1_q_norm_g, l1_k_norm_g, l1_w_out, l2_ada_w, l2_ada_b, l2_pre_g, l2_post_g, l2_w_in, l2_q_a_norm_g, l2_w_q_b, l2_kv_a_norm_g, l2_w_kv_b, l2_w_out, l3_ada_w, l3_ada_b, l3_pre_g, l3_post_g, l3_w_in, l3_lam_q1, l3_lam_k1, l3_lam_q2, l3_lam_k2, l3_subln_g, l3_w_out):
    batch, seq, d = x.shape
    ctx_len = ctx.shape[1]
    assert d == D_MODEL and seq % GRID_W == 0 and batch + 1 <= MOD_ROWS
    diff_names = ("ada_w", "ada_b", "pre_g", "post_g", "w_in", "lam_q1", "lam_k1", "lam_q2",
                  "lam_k2", "subln_g", "w_out")
    layers = [
        ("diff", dict(zip(diff_names, (l0_ada_w, l0_ada_b, l0_pre_g, l0_post_g, l0_w_in,
                                       l0_lam_q1, l0_lam_k1, l0_lam_q2, l0_lam_k2, l0_subln_g,
                                       l0_w_out)))),
        ("gqa", dict(ada_w=l1_ada_w, ada_b=l1_ada_b, pre_g=l1_pre_g, post_g=l1_post_g,
                     w_in=l1_w_in, q_norm_g=l1_q_norm_g, k_norm_g=l1_k_norm_g, w_out=l1_w_out)),
        ("mla", dict(ada_w=l2_ada_w, ada_b=l2_ada_b, pre_g=l2_pre_g, post_g=l2_post_g,
                     w_in=l2_w_in, q_a_norm_g=l2_q_a_norm_g, w_q_b=l2_w_q_b,
                     kv_a_norm_g=l2_kv_a_norm_g, w_kv_b=l2_w_kv_b, w_out=l2_w_out)),
        ("diff", dict(zip(diff_names, (l3_ada_w, l3_ada_b, l3_pre_g, l3_post_g, l3_w_in,
                                       l3_lam_q1, l3_lam_k1, l3_lam_q2, l3_lam_k2, l3_subln_g,
                                       l3_w_out)))),
    ]
    cc = jnp.zeros((MOD_ROWS, d), F32).at[:batch].set(c).at[batch].set(c_ctx)
    h3 = (x, ctx)
    kind0, p0 = layers[0]
    mod = _ada_call(cc, p0["ada_w"], p0["ada_b"])
    to_cast, ready = _prep_weights(kind0, p0)
    wts = dict(ready, **{name: w.astype(BF16) for name, w in to_cast.items()})
    for layer, (kind, p) in enumerate(layers):
        nxt, ready = None, {}
        if layer + 1 < len(layers):
            kind_n, p_n = layers[layer + 1]
            to_cast, ready = _prep_weights(kind_n, p_n)
            nxt = (cc, p_n["ada_w"], p_n["ada_b"], to_cast)
        h3, mod, cast = _layer(kind, layer, h3, mod, wts, p, nxt, seq=seq, ctx_len=ctx_len)
        wts = dict(ready, **(cast or {}))
    return h3
```

```python
import functools
import math

import jax
import jax.numpy as jnp
import numpy as np
from jax import lax
from jax.experimental import pallas as pl
from jax.experimental.pallas import tpu as pltpu

F32 = jnp.float32
BF16 = jnp.bfloat16
LOG2_E = math.log2(math.e)

D_MODEL = 2048
GRID_W = 64
ROPE_BASE = 10000.0
NORM_EPS = 1e-6
DA_HEAD_DIM = 64
DA_HEADS = D_MODEL // (2 * DA_HEAD_DIM)
GQ_HEAD_DIM = 128
GQ_HEADS = D_MODEL // GQ_HEAD_DIM
GQ_KV_HEADS = GQ_HEADS // 4
GQ_KV_W = GQ_KV_HEADS * GQ_HEAD_DIM
ML_NOPE_DIM = 128
ML_ROPE_DIM = 64
ML_V_DIM = 128
ML_HEADS = D_MODEL // ML_V_DIM
ML_Q_RANK = D_MODEL // 4
ML_KV_RANK = D_MODEL // 4

LANES = 128
SUBLANES = 8
MXU_N = 512
ROW_CHUNK = 128
VMEM_LIMIT = 56 * 1024 * 1024
MOD_ROWS = 16

PROJ_TM = 768
MLA_B_TM = 1152
FINISH_TM = 384
FINISH_TM_LAST = 512
PROJ_SUB = 256
FINISH_SUB = 128
MLA_HEADS_PER_STEP = 2
DA_HEADS_PER_STEP = 2
KEY_CHUNK = 256


def _silu(x):
    return x / (1.0 + jnp.exp(-x))


def _rms(x, eps=NORM_EPS):
    return x * lax.rsqrt(jnp.mean(x * x, axis=-1, keepdims=True) + eps)


def _rope(x, cos, sin_a, sin_b, half):
    return (x * cos + pltpu.roll(x, LANES - half, 1) * sin_a + pltpu.roll(x, half, 1) * sin_b)


def _swap_halves(x, half):
    parts = []
    for r in range(0, x.shape[0], 2 * half):
        parts += [x[r + half:r + 2 * half], x[r:r + half]]
    return jnp.concatenate(parts, axis=0)


def _rope_t(x, cos_t, sin_t, half):
    return x * cos_t + _swap_halves(x, half) * sin_t


def _rope_tables(seq, ctx_len, head_dim):
    t = seq + ctx_len
    r = np.arange(t)
    lat = (r < seq)[:, None]
    rowpos = (r // GRID_W).astype(np.float32)[:, None]
    colpos = (r % GRID_W).astype(np.float32)[:, None]
    lane = np.arange(LANES)
    u = lane % head_dim
    half2 = head_dim // 2
    half = half2 // 2
    grp = (u // half2)[None, :]
    w = u % half2
    first = (w < half)[None, :]
    inv_freq = (np.float32(ROPE_BASE) ** (-(w % half).astype(np.float32) / np.float32(half)))[None, :]
    ang = (np.where(grp == 0, rowpos, colpos) * inv_freq).astype(np.float32)
    cos = np.where(lat, np.cos(ang), 1.0).astype(np.float32)
    sin = np.where(lat, np.sin(ang), 0.0).astype(np.float32)
    sin_a = np.where(first, -sin, np.float32(0.0))
    sin_b = np.where(first, np.float32(0.0), sin)
    return cos, sin_a, sin_b


def _rope_tables_t(seq, ctx_len, head_dim, gain, scale):
    cos, sin_a, sin_b = _rope_tables(seq, ctx_len, head_dim)
    half = head_dim // 4
    lane = np.arange(LANES)
    partner = np.where((lane % (2 * half)) < half, lane + half, lane - half)
    cos_t = jnp.asarray(np.ascontiguousarray((cos * np.float32(scale)).T))
    sin_t = jnp.asarray(np.ascontiguousarray(((sin_a + sin_b) * np.float32(scale)).T))
    if gain is None:
        return cos_t, sin_t
    g = jnp.tile(gain.astype(F32), LANES // head_dim)
    return cos_t * g[:, None], sin_t * g[partner][:, None]


def _ada_kernel(c_ref, w_ref, b_ref, o_ref):
    s = _silu(c_ref[...])
    o_ref[...] = jnp.dot(s.astype(BF16), w_ref[...].astype(BF16),
                         preferred_element_type=F32) + b_ref[...]


def _ada_call(cc, ada_w, ada_b):
    d, n = ada_w.shape
    tn = 768
    return pl.pallas_call(
        _ada_kernel,
        grid=(n // tn,),
        in_specs=[pl.BlockSpec((MOD_ROWS, d), lambda j: (0, 0)),
                  pl.BlockSpec((d, tn), lambda j: (0, j)),
                  pl.BlockSpec((1, tn), lambda j: (0, j))],
        out_specs=pl.BlockSpec((MOD_ROWS, tn), lambda j: (0, j)),
        out_shape=jax.ShapeDtypeStruct((MOD_ROWS, n), F32),
        compiler_params=pltpu.CompilerParams(dimension_semantics=("arbitrary",),
                                             vmem_limit_bytes=VMEM_LIMIT),
        name="ada",
    )(cc, ada_w, ada_b.reshape(1, n))


_KIND_OUTS = {"da_q": ("qt",), "gq_q": ("qt",), "da_k": ("o",), "gq_kv": ("o", "vt"),
              "v": ("vt",), "g": ("g",), "ml_qa": ("o",), "ml_kva": ("o",), "ml_kr": ("o",)}


def _mod_rows(mod_b_ref, mod_c_ref, is_ctx, part):
    d = D_MODEL
    mb = mod_b_ref[:, part * d:(part + 1) * d]
    mc = mod_c_ref[:, part * d:(part + 1) * d]
    return jnp.where(is_ctx, mc, mb)


def _h_chunk(h_refs, n_ctx_refs, c, is_ctx):
    if len(h_refs) == 1:
        return h_refs[0][c * ROW_CHUNK:(c + 1) * ROW_CHUNK, :]
    n_sb = len(h_refs) - n_ctx_refs
    per_sb = h_refs[0].shape[0] // ROW_CHUNK
    r, rows = c // per_sb, slice((c % per_sb) * ROW_CHUNK, (c % per_sb + 1) * ROW_CHUNK)
    v = h_refs[r][rows, :]
    if r >= n_sb - n_ctx_refs:
        v = jnp.where(is_ctx, h_refs[n_sb + r - (n_sb - n_ctx_refs)][rows, :], v)
    return v


def _split_h_specs(x, ctx, tm, sub, idx):
    batch, seq, d = x.shape
    ctx_len = ctx.shape[1]
    assert tm % sub == 0 and seq % sub == 0 and ctx_len % sub == 0 and (seq + ctx_len) % tm == 0
    n_sb, last = tm // sub, seq // sub - 1
    xv = x.reshape(batch, seq // sub, sub, d)
    cv = ctx.reshape(batch, ctx_len // sub, sub, d)
    specs = [pl.BlockSpec((None, None, sub, d),
                          lambda *g, r=r: (idx(*g)[0], jnp.minimum(idx(*g)[1] * n_sb + r, last), 0, 0))
             for r in range(n_sb)]
    specs += [pl.BlockSpec((None, None, sub, d), lambda *g, q=q: (idx(*g)[0], q, 0, 0))
              for q in range(ctx_len // sub)]
    return [xv] * n_sb + [cv] * (ctx_len // sub), specs, ctx_len // sub


def _prologue(h_refs, n_ctx_refs, mod_b_ref, mod_c_ref, preg_ref, xn_ref, slot, row0, seq, pieces):
    preg = preg_ref[...]
    for c in pieces:
        rows = pl.ds(c * ROW_CHUNK, ROW_CHUNK)
        is_ctx = (row0 + c * ROW_CHUNK) >= seq
        shift = _mod_rows(mod_b_ref, mod_c_ref, is_ctx, 0)
        scale = _mod_rows(mod_b_ref, mod_c_ref, is_ctx, 1)
        y = _rms(_h_chunk(h_refs, n_ctx_refs, c, is_ctx)) * preg
        xn_ref[slot, rows, :] = (y * (1.0 + scale) + shift).astype(BF16)


def _proj_kernel(*refs, kinds, out_names, n_h, n_ctx_refs, n_tabs, seq, n_tiles, tiles_per_batch,
                 rope_half):
    h_refs, refs = refs[:n_h], refs[n_h:]
    mod_b_ref, mod_c_ref, preg_ref, w_ref = refs[:4]
    tabs = refs[4:4 + n_tabs]
    ng0_ref, ng1_ref = refs[4 + n_tabs:6 + n_tabs]
    outs = dict(zip(out_names, refs[6 + n_tabs:]))
    xn_ref = refs[-1]
    i = pl.program_id(0)
    j = pl.program_id(1)
    tm = xn_ref.shape[1]
    n_steps = len(kinds)
    n_pieces = tm // ROW_CHUNK
    row0 = (jnp.minimum(i, n_tiles - 1) % tiles_per_batch) * tm
    prologue = functools.partial(_prologue, h_refs, n_ctx_refs, mod_b_ref, mod_c_ref, preg_ref,
                                 xn_ref, i % 2, row0, seq)

    @pl.when((i == 0) & (j == 0))
    def _():
        prologue(range(n_pieces))

    def heads_t(a):
        at = a.T
        return [at[c * LANES:(c + 1) * LANES] for c in range(at.shape[0] // LANES)]

    def head_rms(xh):
        return lax.rsqrt(jnp.mean(xh * xh, axis=0, keepdims=True) + NORM_EPS)

    def tile(kind):
        a = jnp.dot(xn_ref[(i + 1) % 2], w_ref[...], preferred_element_type=F32)
        if kind == "g":
            outs["g"][...] = a
        elif kind == "v":
            outs["vt"][...] = a.T.astype(BF16)
        elif kind in ("da_q", "gq_q"):
            cos_t, sin_t = tabs[0][...], tabs[1][...]
            for c, xh in enumerate(heads_t(a)):
                y = _rope_t(xh, cos_t, sin_t, rope_half)
                if kind == "gq_q":
                    y = y * head_rms(xh)
                outs["qt"][c * LANES:(c + 1) * LANES, :] = y.astype(BF16)
        elif kind in ("da_k", "gq_kv"):
            cos_t, sin_t = tabs[2][...], tabs[3][...]
            k_cols = outs["o"].shape[1]
            for c, xh in enumerate(heads_t(a[:, :k_cols])):
                y = _rope_t(xh, cos_t, sin_t, rope_half)
                if kind == "gq_kv":
                    y = y * head_rms(xh)
                outs["o"][:, c * LANES:(c + 1) * LANES] = y.T.astype(BF16)
            if kind == "gq_kv":
                outs["vt"][...] = a[:, k_cols:].T.astype(BF16)
        elif kind == "ml_qa":
            outs["o"][...] = (_rms(a) * ng0_ref[...]).astype(BF16)
        elif kind == "ml_kva":
            outs["o"][...] = (_rms(a) * ng1_ref[...]).astype(BF16)
        elif kind == "ml_kr":
            outs["o"][...] = a.astype(BF16)
            outs["o"][:, :LANES] = _rope(a[:, :LANES], tabs[0][...], tabs[1][...], tabs[2][...],
                                         rope_half).astype(BF16)
        else:
            raise ValueError(kind)

    def step(jv):
        tile(kinds[jv])
        prologue([p for p in range(n_pieces) if n_steps - 1 - p % n_steps == jv])

    for jv in range(n_steps):
        pl.when((i > 0) & (j == jv))(functools.partial(step, jv))


def _proj_call(h, mod3, pre_g, w, tabs, ng0, ng1, *, kinds, tn, o_cols, v_rows, tabs_transposed,
               batch, seq, ctx_len, rope_half):
    d = D_MODEL
    m = batch * (seq + ctx_len)
    n = w.shape[1]
    t = seq + ctx_len
    tm = PROJ_TM
    tpb = t // tm
    n_tiles = m // tm
    assert t % tm == 0 and n % tn == 0 and len(kinds) == n // tn

    def nxt(i):
        return jnp.minimum(i, n_tiles - 1)

    def cur(i):
        return jnp.maximum(i - 1, 0)

    def tile_idx(name):
        js = [jv for jv, k in enumerate(kinds) if name in _KIND_OUTS[k]]
        assert js == list(range(js[0], js[-1] + 1))
        return lambda i, j: jnp.where(i == 0, 0, jnp.clip(j - js[0], 0, len(js) - 1)), len(js)

    out_names, out_specs, out_shape = [], [], []
    for name in ("qt", "o", "vt", "g"):
        if not any(name in _KIND_OUTS[k] for k in kinds):
            continue
        idx, cnt = tile_idx(name)
        out_names.append(name)
        if name in ("qt", "vt"):
            rows = tn if name == "qt" else v_rows
            out_specs.append(pl.BlockSpec((rows, tm), lambda i, j, idx=idx: (idx(i, j), cur(i))))
            out_shape.append(jax.ShapeDtypeStruct((cnt * rows, m), BF16))
        else:
            cols, dt = (o_cols, BF16) if name == "o" else (tn, F32)
            out_specs.append(pl.BlockSpec((tm, cols), lambda i, j, idx=idx: (cur(i), idx(i, j))))
            out_shape.append(jax.ShapeDtypeStruct((m, cnt * cols), dt))
    assert out_shape[-1].shape == (m, d)

    if tabs_transposed:
        tab_spec = pl.BlockSpec((LANES, tm), lambda i, j: (0, cur(i) % tpb))
    else:
        tab_spec = pl.BlockSpec((tm, LANES), lambda i, j: (cur(i) % tpb, 0))
    if isinstance(h, tuple):
        h_args, h_specs, n_ctx_refs = _split_h_specs(
            *h, tm, PROJ_SUB, lambda i, j: (nxt(i) // tpb, nxt(i) % tpb))
    else:
        h_args, h_specs, n_ctx_refs = [h], [pl.BlockSpec((tm, d), lambda i, j: (nxt(i), 0))], 0
    kern = functools.partial(_proj_kernel, kinds=tuple(kinds), out_names=tuple(out_names),
                             n_h=len(h_args), n_ctx_refs=n_ctx_refs, n_tabs=len(tabs), seq=seq,
                             n_tiles=n_tiles, tiles_per_batch=tpb, rope_half=rope_half)
    return pl.pallas_call(
        kern,
        grid=(n_tiles + 1, n // tn),
        in_specs=h_specs + [
            pl.BlockSpec((None, 1, 3 * d), lambda i, j: (nxt(i) // tpb, 0, 0)),
            pl.BlockSpec((None, 1, 3 * d), lambda i, j: (batch, 0, 0)),
            pl.BlockSpec((1, d), lambda i, j: (0, 0)),
            pl.BlockSpec((d, tn), lambda i, j: (0, jnp.where(i == 0, 0, j))),
        ] + [tab_spec] * len(tabs) + [
            pl.BlockSpec(ng0.shape, lambda i, j: (0, 0)),
            pl.BlockSpec(ng1.shape, lambda i, j: (0, 0)),
        ],
        out_specs=out_specs,
        out_shape=out_shape,
        scratch_shapes=[pltpu.VMEM((2, tm, d), BF16)],
        compiler_params=pltpu.CompilerParams(dimension_semantics=("arbitrary", "arbitrary"),
                                             vmem_limit_bytes=VMEM_LIMIT),
        name="proj",
    )(*h_args, mod3, mod3, pre_g.reshape(1, d), w, *tabs, ng0, ng1)


def _mla_b_kernel(qa_ref, kva_ref, kr_ref, wq_ref, wkn_ref, wv_ref, cos_t_ref, sin_t_ref,
                  qt_ref, k_ref, vt_ref, *, q_scale):
    qt = jnp.dot(qa_ref[...], wq_ref[...], preferred_element_type=F32).T
    for c in range(qt.shape[0] // LANES):
        x = qt[c * LANES:(c + 1) * LANES]
        if c % 2 == 1:
            x = _rope_t(x, cos_t_ref[...], sin_t_ref[...], ML_ROPE_DIM // 4)
        else:
            x = x * q_scale
        qt_ref[c * LANES:(c + 1) * LANES, :] = x.astype(BF16)
    kva = kva_ref[...]
    kn = jnp.dot(kva, wkn_ref[...], preferred_element_type=F32)
    kr = kr_ref[...]
    for hh in range(kn.shape[1] // LANES):
        k_ref[:, 2 * hh * LANES:(2 * hh + 1) * LANES] = kn[:, hh * LANES:(hh + 1) * LANES].astype(BF16)
        k_ref[:, (2 * hh + 1) * LANES:(2 * hh + 2) * LANES] = kr
    vt_ref[...] = jnp.dot(kva, wv_ref[...], preferred_element_type=F32).T.astype(BF16)


def _mla_b_call(qkv, wq, wkv, tabs_t, *, seq, ctx_len, q_scale):
    m = qkv.shape[0]
    tm = MLA_B_TM
    tpb = (seq + ctx_len) // tm
    assert (seq + ctx_len) % tm == 0
    r = ML_Q_RANK
    nq = wq.shape[1]
    nkv = wkv.shape[1] // 2
    steps = 4
    tq, tk = nq // steps, nkv // steps
    kr_col = (ML_Q_RANK + ML_KV_RANK) // LANES
    tab_spec = pl.BlockSpec((LANES, tm), lambda i, j: (0, i % tpb))
    return pl.pallas_call(
        functools.partial(_mla_b_kernel, q_scale=q_scale),
        grid=(m // tm, steps),
        in_specs=[
            pl.BlockSpec((tm, r), lambda i, j: (i, 0)),
            pl.BlockSpec((tm, r), lambda i, j: (i, 1)),
            pl.BlockSpec((tm, LANES), lambda i, j: (i, kr_col)),
            pl.BlockSpec((r, tq), lambda i, j: (0, j)),
            pl.BlockSpec((r, tk), lambda i, j: (0, j)),
            pl.BlockSpec((r, tk), lambda i, j: (0, steps + j)),
            tab_spec, tab_spec,
        ],
        out_specs=[
            pl.BlockSpec((tq, tm), lambda i, j: (j, i)),
            pl.BlockSpec((tm, 2 * tk), lambda i, j: (i, j)),
            pl.BlockSpec((tk, tm), lambda i, j: (j, i)),
        ],
        out_shape=[jax.ShapeDtypeStruct((nq, m), BF16),
                   jax.ShapeDtypeStruct((m, 2 * nkv), BF16),
                   jax.ShapeDtypeStruct((nkv, m), BF16)],
        compiler_params=pltpu.CompilerParams(dimension_semantics=("arbitrary", "arbitrary"),
                                             vmem_limit_bytes=VMEM_LIMIT),
        name="mla_b",
    )(qkv, qkv, qkv, wq, wkv, wkv, *tabs_t)


def _sublane_groups(x):
    return x.reshape(x.shape[0] // SUBLANES, SUBLANES, x.shape[1])


def _run_chains(chains, s_scr, e_scr):
    n = len(chains)
    shape, m, l = {}, {}, {}
    for step in range(n + 2):
        if step < n:
            qt, k, vt, _ = chains[step]
            qv = qt()
            nq, nk = qv.shape[1], vt().shape[1]
            shape[step] = (nk, nq)
            kc = min(KEY_CHUNK, nk)
            m8 = None
            for c0 in range(0, nk, kc):
                st = jnp.dot(k(c0, kc), qv, preferred_element_type=F32)
                s_scr[step % 2, c0:c0 + kc, :nq] = st
                cm = jnp.max(_sublane_groups(st), axis=0)
                m8 = cm if m8 is None else jnp.maximum(m8, cm)
            m[step] = jnp.max(m8, axis=0, keepdims=True)
        i = step - 1
        if 0 <= i < n:
            nk, nq = shape[i]
            mi = m.pop(i)
            kc = min(KEY_CHUNK, nk)
            l8 = None
            for c0 in range(0, nk, kc):
                p = jnp.exp2(s_scr[i % 2, c0:c0 + kc, :nq] - mi)
                ps = jnp.sum(_sublane_groups(p), axis=0)
                l8 = ps if l8 is None else l8 + ps
                e_scr[i % 2, c0:c0 + kc, :nq] = p.astype(BF16)
            l[i] = jnp.sum(l8, axis=0, keepdims=True)
        i = step - 2
        if 0 <= i < n:
            nk, nq = shape[i]
            _, _, vt, done = chains[i]
            done(jnp.dot(vt(), e_scr[i % 2, :nk, :nq], preferred_element_type=F32) / l.pop(i))


def _row_blocks(rows_q, seq):
    assert seq % MXU_N == 0
    blocks = [(r0, MXU_N, 0) for r0 in range(0, seq, MXU_N)]
    if rows_q > seq:
        blocks.append((seq, rows_q - seq, seq))
    return blocks


def _diff_attn_kernel(lq1_ref, lk1_ref, lq2_ref, lk2_ref, subg_ref, qt_ref, k_ref, vt_ref, g_ref,
                      o_ref, s_scr, e_scr, *, seq, lambda_init):
    lam = (jnp.exp(jnp.sum(lq1_ref[...] * lk1_ref[...], axis=-1, keepdims=True))
           - jnp.exp(jnp.sum(lq2_ref[...] * lk2_ref[...], axis=-1, keepdims=True))
           + lambda_init)
    chains = []
    for r0, nr, key0 in _row_blocks(o_ref.shape[0], seq):
        rows = slice(r0, r0 + nr)
        for hh in range(qt_ref.shape[0] // LANES):
            cols = slice(hh * LANES, (hh + 1) * LANES)
            parts = []

            def q_sub(sub, rows=rows, cols=cols):
                qf = qt_ref[cols, rows].astype(F32)
                row = lax.broadcasted_iota(jnp.int32, qf.shape, 0)
                return jnp.where((row >= DA_HEAD_DIM) == bool(sub), qf, 0.0).astype(BF16)

            def done(ot, rows=rows, cols=cols, parts=parts):
                parts.append(ot)
                if len(parts) == 2:
                    o = (parts[0] - lam * parts[1]).T
                    o = _rms(o) * subg_ref[...] * (1.0 - lambda_init)
                    o_ref[rows, cols] = (o * _silu(g_ref[rows, cols])).astype(BF16)

            for sub in range(2):
                chains.append((functools.partial(q_sub, sub),
                               lambda c0, kc, key0=key0, cols=cols: k_ref[key0 + c0:key0 + c0 + kc,
                                                                          cols],
                               lambda key0=key0, cols=cols: vt_ref[cols, key0:], done))
    _run_chains(chains, s_scr, e_scr)


def _plain_attn_kernel(qt_ref, k_ref, vt_ref, g_ref, o_ref, s_scr, e_scr, *, seq, dq, shared_kv):
    n_heads = qt_ref.shape[0] // dq
    chains = []
    for r0, nr, key0 in _row_blocks(o_ref.shape[0], seq):
        rows = slice(r0, r0 + nr)
        for hh in range(n_heads):
            kh = 0 if shared_kv else hh
            cols = slice(hh * LANES, (hh + 1) * LANES)

            def done(ot, rows=rows, cols=cols):
                o_ref[rows, cols] = (ot.T * _silu(g_ref[rows, cols])).astype(BF16)

            chains.append((lambda rows=rows, hh=hh: qt_ref[hh * dq:(hh + 1) * dq, rows],
                           lambda c0, kc, key0=key0, kh=kh: k_ref[key0 + c0:key0 + c0 + kc,
                                                                  kh * dq:(kh + 1) * dq],
                           lambda key0=key0, kh=kh: vt_ref[kh * LANES:(kh + 1) * LANES, key0:],
                           done))
    _run_chains(chains, s_scr, e_scr)


def _attn_call(kind, qt, k3, vt, g3, small, *, seq, need_ctx, lambda_init=None):
    batch, t, _ = k3.shape
    rows = t if need_ctx else seq
    heads = D_MODEL // LANES
    if kind == "diff":
        hps, kv_heads, dq, k_col0 = DA_HEADS_PER_STEP, DA_HEADS_PER_STEP, LANES, 0
        kern = functools.partial(_diff_attn_kernel, seq=seq, lambda_init=lambda_init)
    elif kind == "gqa":
        hps, kv_heads, dq, k_col0 = GQ_HEADS // GQ_KV_HEADS, 1, LANES, 0
        kern = functools.partial(_plain_attn_kernel, seq=seq, dq=dq, shared_kv=True)
    elif kind == "mla":
        hps, kv_heads, dq, k_col0 = MLA_HEADS_PER_STEP, MLA_HEADS_PER_STEP, 2 * LANES, 0
        kern = functools.partial(_plain_attn_kernel, seq=seq, dq=dq, shared_kv=False)
    else:
        raise ValueError(kind)
    kw = kv_heads * dq
    in_specs = [pl.BlockSpec(a.shape, lambda b, s: (0,) * a.ndim) for a in small] + [
        pl.BlockSpec((hps * dq, t), lambda b, s: (s, b)),
        pl.BlockSpec((None, t, kw), lambda b, s: (b, 0, k_col0 + s)),
        pl.BlockSpec((kv_heads * LANES, t), lambda b, s: (s, b)),
        pl.BlockSpec((None, rows, hps * LANES), lambda b, s: (b, 0, s)),
    ]
    return pl.pallas_call(
        kern,
        grid=(batch, heads // hps),
        in_specs=in_specs,
        out_specs=pl.BlockSpec((None, rows, hps * LANES), lambda b, s: (b, 0, s)),
        out_shape=jax.ShapeDtypeStruct((batch, rows, D_MODEL), BF16),
        scratch_shapes=[pltpu.VMEM((2, t, MXU_N), F32), pltpu.VMEM((2, t, MXU_N), BF16)],
        compiler_params=pltpu.CompilerParams(dimension_semantics=("arbitrary", "arbitrary"),
                                             vmem_limit_bytes=VMEM_LIMIT),
        name="attn_" + kind,
    )(*small, qt, k3, vt, g3)


def _finish_kernel(*refs, seq, n_h, n_ctx_refs, n_cast):
    h_refs, refs = refs[:n_h], refs[n_h:]
    a_ref, mod_b_ref, mod_c_ref, postg_ref, w_ref = refs[:5]
    n_in = 5 + (3 + n_cast if n_cast else 0)
    o_ref = refs[n_in]
    ti = pl.program_id(1)
    tm = a_ref.shape[0]
    y = jnp.dot(a_ref[...], w_ref[...], preferred_element_type=F32)
    postg = postg_ref[...]
    for c in range(tm // ROW_CHUNK):
        rows = slice(c * ROW_CHUNK, (c + 1) * ROW_CHUNK)
        is_ctx = (ti * tm + c * ROW_CHUNK) >= seq
        gate = _mod_rows(mod_b_ref, mod_c_ref, is_ctx, 2)
        o_ref[rows, :] = (_h_chunk(h_refs, n_ctx_refs, c, is_ctx)
                          + gate * (_rms(y[rows, :]) * postg))
    if n_cast:
        c_ref, adaw_ref, adab_ref = refs[5:8]
        _ada_kernel(c_ref, adaw_ref, adab_ref, refs[n_in + 1])
        for src, dst in zip(refs[8:n_in], refs[n_in + 2:]):
            dst[...] = src[...].astype(BF16)


def _finish_call(a3, h, mod3, post_g, w, *, seq, nxt=None):
    batch, rows_out, d = a3.shape
    t = sum(v.shape[1] for v in h) if isinstance(h, tuple) else h.shape[1]
    tm = FINISH_TM if rows_out == t else FINISH_TM_LAST
    assert rows_out % tm == 0
    n_i = rows_out // tm
    blk = pl.BlockSpec((None, tm, d), lambda b, i: (b, i, 0))
    if isinstance(h, tuple):
        h_args, h_specs, n_ctx_refs = _split_h_specs(*h, tm, FINISH_SUB, lambda b, i: (b, i))
    else:
        h_args, h_specs, n_ctx_refs = [h], [blk], 0
    in_specs = h_specs + [
        blk,
        pl.BlockSpec((None, 1, 3 * d), lambda b, i: (b, 0, 0)),
        pl.BlockSpec((None, 1, 3 * d), lambda b, i: (batch, 0, 0)),
        pl.BlockSpec((1, d), lambda b, i: (0, 0)),
        pl.BlockSpec((d, d), lambda b, i: (0, 0)),
    ]
    args = h_args + [a3, mod3, mod3, post_g.reshape(1, d), w]
    out_specs = [blk]
    out_shape = [jax.ShapeDtypeStruct((batch, rows_out, d), F32)]
    names = []
    if nxt is not None:
        cc, ada_w, ada_b, weights = nxt
        n_mod = ada_w.shape[1]
        assert n_mod == batch * n_i * LANES
        flat = lambda b, i: (0, b * n_i + i)
        in_specs += [pl.BlockSpec((MOD_ROWS, d), lambda b, i: (0, 0)),
                     pl.BlockSpec((d, LANES), flat), pl.BlockSpec((1, LANES), flat)]
        args += [cc, ada_w, ada_b.reshape(1, n_mod)]
        out_specs.append(pl.BlockSpec((MOD_ROWS, LANES), flat))
        out_shape.append(jax.ShapeDtypeStruct((MOD_ROWS, n_mod), F32))
        for name, wf in weights.items():
            r, c = wf.shape
            assert r % (batch * 2 * SUBLANES) == 0 and c % (4 * LANES) == 0 and n_i >= 4
            spec = pl.BlockSpec((r // batch, c // 4), lambda b, i: (b, jnp.minimum(i, 3)))
            names.append(name)
            in_specs.append(spec)
            args.append(wf)
            out_specs.append(spec)
            out_shape.append(jax.ShapeDtypeStruct((r, c), BF16))
    outs = pl.pallas_call(
        functools.partial(_finish_kernel, seq=seq, n_h=len(h_args), n_ctx_refs=n_ctx_refs,
                          n_cast=len(names)),
        grid=(batch, n_i),
        in_specs=in_specs,
        out_specs=out_specs,
        out_shape=out_shape,
        compiler_params=pltpu.CompilerParams(dimension_semantics=("arbitrary", "arbitrary"),
                                             vmem_limit_bytes=VMEM_LIMIT),
        name="finish",
    )(*args)
    if nxt is None:
        return outs[0], None, None
    return outs[0], outs[1], dict(zip(names, outs[2:]))


def _lambda_init(layer):
    return 0.8 - 0.6 * math.exp(-0.3 * layer)


def _prep_weights(kind, p):
    if kind != "mla":
        return dict(w_in=p["w_in"], w_out=p["w_out"]), {}
    w_in = p["w_in"].astype(BF16)
    d = w_in.shape[0]
    r2 = ML_Q_RANK + ML_KV_RANK
    ready = {}
    ready["w_in"] = jnp.concatenate(
        [w_in[:, :r2 + ML_ROPE_DIM], jnp.zeros((d, ML_Q_RANK - ML_ROPE_DIM), BF16),
         w_in[:, r2 + ML_ROPE_DIM:]], axis=1)
    qd = ML_NOPE_DIM + ML_ROPE_DIM
    wq = p["w_q_b"].astype(BF16).reshape(ML_Q_RANK, ML_HEADS, qd)
    wq = jnp.pad(wq, ((0, 0), (0, 0), (0, 2 * LANES - qd)))
    ready["wq"] = wq.reshape(ML_Q_RANK, ML_HEADS * 2 * LANES)
    wkv = p["w_kv_b"].astype(BF16).reshape(ML_KV_RANK, ML_HEADS, ML_NOPE_DIM + ML_V_DIM)
    ready["wkv"] = jnp.concatenate([wkv[:, :, :ML_NOPE_DIM].reshape(ML_KV_RANK, -1),
                                    wkv[:, :, ML_NOPE_DIM:].reshape(ML_KV_RANK, -1)], axis=1)
    return dict(w_out=p["w_out"]), ready


def _layer(kind, layer, h3, mod, wts, p, nxt, *, seq, ctx_len):
    d = D_MODEL
    t = seq + ctx_len
    batch = h3[0].shape[0] if isinstance(h3, tuple) else h3.shape[0]
    need_ctx = nxt is not None
    mod3 = mod.reshape(MOD_ROWS, 1, 3 * d)
    h2 = h3 if isinstance(h3, tuple) else h3.reshape(batch * t, d)
    one = jnp.ones((1, LANES), F32)
    common = dict(batch=batch, seq=seq, ctx_len=ctx_len)
    tables_t = functools.partial(_rope_tables_t, seq, ctx_len)

    def b3(a):
        return a.reshape(batch, t, a.shape[-1])

    if kind == "diff":
        tabs = (tables_t(DA_HEAD_DIM, None, DA_HEAD_DIM ** -0.5 * LOG2_E)
                + tables_t(DA_HEAD_DIM, None, 1.0))
        qt, k, vt, g = _proj_call(h2, mod3, p["pre_g"], wts["w_in"], tabs, one, one,
                                  kinds=["da_q"] * 2 + ["da_k"] * 2 + ["v"] * 2 + ["g"] * 2,
                                  tn=1024, o_cols=1024, v_rows=1024, tabs_transposed=True,
                                  rope_half=DA_HEAD_DIM // 4, **common)
        small = [p[n].reshape(1, DA_HEAD_DIM) for n in ("lam_q1", "lam_k1", "lam_q2", "lam_k2")]
        small.append(p["subln_g"].reshape(1, 2 * DA_HEAD_DIM))
        a3 = _attn_call("diff", qt, b3(k), vt, b3(g), small, seq=seq, need_ctx=need_ctx,
                        lambda_init=_lambda_init(layer))
    elif kind == "gqa":
        tabs = (tables_t(GQ_HEAD_DIM, p["q_norm_g"], GQ_HEAD_DIM ** -0.5 * LOG2_E)
                + tables_t(GQ_HEAD_DIM, p["k_norm_g"], 1.0))
        qt, k, vt, g = _proj_call(h2, mod3, p["pre_g"], wts["w_in"], tabs, one, one,
                                  kinds=["gq_q"] * 2 + ["gq_kv"] + ["g"] * 2,
                                  tn=1024, o_cols=GQ_KV_W, v_rows=GQ_KV_W, tabs_transposed=True,
                                  rope_half=GQ_HEAD_DIM // 4, **common)
        a3 = _attn_call("gqa", qt, b3(k), vt, b3(g), [], seq=seq, need_ctx=need_ctx)
    elif kind == "mla":
        tn = ML_Q_RANK
        qkv, g = _proj_call(h2, mod3, p["pre_g"], wts["w_in"],
                            _rope_tables(seq, ctx_len, ML_ROPE_DIM),
                            p["q_a_norm_g"].reshape(1, ML_Q_RANK),
                            p["kv_a_norm_g"].reshape(1, ML_KV_RANK),
                            kinds=["ml_qa", "ml_kva", "ml_kr"] + ["g"] * (d // tn),
                            tn=tn, o_cols=tn, v_rows=0, tabs_transposed=False,
                            rope_half=ML_ROPE_DIM // 4, **common)
        qd = ML_NOPE_DIM + ML_ROPE_DIM
        q_scale = qd ** -0.5 * LOG2_E
        qt, k, vt = _mla_b_call(qkv, wts["wq"], wts["wkv"], tables_t(ML_ROPE_DIM, None, q_scale),
                                seq=seq, ctx_len=ctx_len, q_scale=q_scale)
        a3 = _attn_call("mla", qt, b3(k), vt, b3(g), [], seq=seq, need_ctx=need_ctx)
    else:
        raise ValueError(kind)
    return _finish_call(a3, h3, mod3, p["post_g"], wts["w_out"], seq=seq, nxt=nxt)


def kernel(x, c, ctx, c_ctx, l0_ada_w, l0_ada_b, l0_pre_g, l0_post_g, l0_w_in, l0_lam_q1, l0_lam_k1, l0_lam_q2, l0_lam_k2, l0_subln_g, l0_w_out, l1_ada_w, l1_ada_b, l1_pre_g, l1_post_g, l1_w_in, l1_q_norm_g, l1_k_norm_g, l1_w_out, l2_ada_w, l2_ada_b, l2_pre_g, l2_post_g, l2_w_in, l2_q_a_norm_g, l2_w_q_b, l2_kv_a_norm_g, l2_w_kv_b, l2_w_out, l3_ada_w, l3_ada_b, l3_pre_g, l3_post_g, l3_w_in, l3_lam_q1, l3_lam_k1, l3_lam_q2, l3_lam_k2, l3_subln_g, l3_w_out):
    batch, seq, d = x.shape
    ctx_len = ctx.shape[1]
    assert d == D_MODEL and seq % GRID_W == 0 and batch + 1 <= MOD_ROWS
    diff_names = ("ada_w", "ada_b", "pre_g", "post_g", "w_in", "lam_q1", "lam_k1", "lam_q2",
                  "lam_k2", "subln_g", "w_out")
    layers = [
        ("diff", dict(zip(diff_names, (l0_ada_w, l0_ada_b, l0_pre_g, l0_post_g, l0_w_in,
                                       l0_lam_q1, l0_lam_k1, l0_lam_q2, l0_lam_k2, l0_subln_g,
                                       l0_w_out)))),
        ("gqa", dict(ada_w=l1_ada_w, ada_b=l1_ada_b, pre_g=l1_pre_g, post_g=l1_post_g,
                     w_in=l1_w_in, q_norm_g=l1_q_norm_g, k_norm_g=l1_k_norm_g, w_out=l1_w_out)),
        ("mla", dict(ada_w=l2_ada_w, ada_b=l2_ada_b, pre_g=l2_pre_g, post_g=l2_post_g,
                     w_in=l2_w_in, q_a_norm_g=l2_q_a_norm_g, w_q_b=l2_w_q_b,
                     kv_a_norm_g=l2_kv_a_norm_g, w_kv_b=l2_w_kv_b, w_out=l2_w_out)),
        ("diff", dict(zip(diff_names, (l3_ada_w, l3_ada_b, l3_pre_g, l3_post_g, l3_w_in,
                                       l3_lam_q1, l3_lam_k1, l3_lam_q2, l3_lam_k2, l3_subln_g,
                                       l3_w_out)))),
    ]
    cc = jnp.zeros((MOD_ROWS, d), F32).at[:batch].set(c).at[batch].set(c_ctx)
    h3 = (x, ctx)
    kind0, p0 = layers[0]
    mod = _ada_call(cc, p0["ada_w"], p0["ada_b"])
    to_cast, ready = _prep_weights(kind0, p0)
    wts = dict(ready, **{name: w.astype(BF16) for name, w in to_cast.items()})
    for layer, (kind, p) in enumerate(layers):
        nxt, ready = None, {}
        if layer + 1 < len(layers):
            kind_n, p_n = layers[layer + 1]
            to_cast, ready = _prep_weights(kind_n, p_n)
            nxt = (cc, p_n["ada_w"], p_n["ada_b"], to_cast)
        h3, mod, cast = _layer(kind, layer, h3, mod, wts, p, nxt, seq=seq, ctx_len=ctx_len)
        wts = dict(ready, **(cast or {}))
    return h3
```

```python
import functools
import math

import jax
import jax.numpy as jnp
import numpy as np
from jax import lax
from jax.experimental import pallas as pl
from jax.experimental.pallas import tpu as pltpu

F32 = jnp.float32
BF16 = jnp.bfloat16
LOG2_E = math.log2(math.e)

D_MODEL = 2048
GRID_W = 64
ROPE_BASE = 10000.0
NORM_EPS = 1e-6
DA_HEAD_DIM = 64
DA_HEADS = D_MODEL // (2 * DA_HEAD_DIM)
GQ_HEAD_DIM = 128
GQ_HEADS = D_MODEL // GQ_HEAD_DIM
GQ_KV_HEADS = GQ_HEADS // 4
GQ_KV_W = GQ_KV_HEADS * GQ_HEAD_DIM
ML_NOPE_DIM = 128
ML_ROPE_DIM = 64
ML_V_DIM = 128
ML_HEADS = D_MODEL // ML_V_DIM
ML_Q_RANK = D_MODEL // 4
ML_KV_RANK = D_MODEL // 4

LANES = 128
SUBLANES = 8
MXU_N = 512
ROW_CHUNK = 128
VMEM_LIMIT = 56 * 1024 * 1024
MOD_ROWS = 16

PROJ_TM = 768
MLA_B_TM = 2304
FINISH_TM = 384
FINISH_TM_LAST = 512
PROJ_SUB = 256
FINISH_SUB = 128
MLA_HEADS_PER_STEP = 2
DA_HEADS_PER_STEP = 2
KEY_CHUNK = 256


def _silu(x):
    return x / (1.0 + jnp.exp(-x))


def _rms(x, eps=NORM_EPS):
    return x * lax.rsqrt(jnp.mean(x * x, axis=-1, keepdims=True) + eps)


def _rope(x, cos, sin_a, sin_b, half):
    return (x * cos + pltpu.roll(x, LANES - half, 1) * sin_a + pltpu.roll(x, half, 1) * sin_b)


def _swap_halves(x, half):
    parts = []
    for r in range(0, x.shape[0], 2 * half):
        parts += [x[r + half:r + 2 * half], x[r:r + half]]
    return jnp.concatenate(parts, axis=0)


def _rope_t(x, cos_t, sin_t, half):
    return x * cos_t + _swap_halves(x, half) * sin_t


def _rope_tables(seq, ctx_len, head_dim):
    t = seq + ctx_len
    r = np.arange(t)
    lat = (r < seq)[:, None]
    rowpos = (r // GRID_W).astype(np.float32)[:, None]
    colpos = (r % GRID_W).astype(np.float32)[:, None]
    lane = np.arange(LANES)
    u = lane % head_dim
    half2 = head_dim // 2
    half = half2 // 2
    grp = (u // half2)[None, :]
    w = u % half2
    first = (w < half)[None, :]
    inv_freq = (np.float32(ROPE_BASE) ** (-(w % half).astype(np.float32) / np.float32(half)))[None, :]
    ang = (np.where(grp == 0, rowpos, colpos) * inv_freq).astype(np.float32)
    cos = np.where(lat, np.cos(ang), 1.0).astype(np.float32)
    sin = np.where(lat, np.sin(ang), 0.0).astype(np.float32)
    sin_a = np.where(first, -sin, np.float32(0.0))
    sin_b = np.where(first, np.float32(0.0), sin)
    return cos, sin_a, sin_b


def _rope_tables_t(seq, ctx_len, head_dim, gain, scale):
    cos, sin_a, sin_b = _rope_tables(seq, ctx_len, head_dim)
    half = head_dim // 4
    lane = np.arange(LANES)
    partner = np.where((lane % (2 * half)) < half, lane + half, lane - half)
    cos_t = jnp.asarray(np.ascontiguousarray((cos * np.float32(scale)).T))
    sin_t = jnp.asarray(np.ascontiguousarray(((sin_a + sin_b) * np.float32(scale)).T))
    if gain is None:
        return cos_t, sin_t
    g = jnp.tile(gain.astype(F32), LANES // head_dim)
    return cos_t * g[:, None], sin_t * g[partner][:, None]


def _ada_kernel(c_ref, w_ref, b_ref, o_ref):
    s = _silu(c_ref[...])
    o_ref[...] = jnp.dot(s.astype(BF16), w_ref[...].astype(BF16),
                         preferred_element_type=F32) + b_ref[...]


def _ada_call(cc, ada_w, ada_b):
    d, n = ada_w.shape
    tn = 768
    return pl.pallas_call(
        _ada_kernel,
        grid=(n // tn,),
        in_specs=[pl.BlockSpec((MOD_ROWS, d), lambda j: (0, 0)),
                  pl.BlockSpec((d, tn), lambda j: (0, j)),
                  pl.BlockSpec((1, tn), lambda j: (0, j))],
        out_specs=pl.BlockSpec((MOD_ROWS, tn), lambda j: (0, j)),
        out_shape=jax.ShapeDtypeStruct((MOD_ROWS, n), F32),
        compiler_params=pltpu.CompilerParams(dimension_semantics=("arbitrary",),
                                             vmem_limit_bytes=VMEM_LIMIT),
        name="ada",
    )(cc, ada_w, ada_b.reshape(1, n))


_KIND_OUTS = {"da_q": ("qt",), "gq_q": ("qt",), "da_k": ("o",), "gq_kv": ("o", "vt"),
              "v": ("vt",), "g": ("g",), "ml_qa": ("o",), "ml_kva": ("o",), "ml_kr": ("o",)}


def _mod_rows(mod_b_ref, mod_c_ref, is_ctx, part):
    d = D_MODEL
    mb = mod_b_ref[:, part * d:(part + 1) * d]
    mc = mod_c_ref[:, part * d:(part + 1) * d]
    return jnp.where(is_ctx, mc, mb)


def _h_chunk(h_refs, n_ctx_refs, c, is_ctx):
    if len(h_refs) == 1:
        return h_refs[0][c * ROW_CHUNK:(c + 1) * ROW_CHUNK, :]
    n_sb = len(h_refs) - n_ctx_refs
    per_sb = h_refs[0].shape[0] // ROW_CHUNK
    r, rows = c // per_sb, slice((c % per_sb) * ROW_CHUNK, (c % per_sb + 1) * ROW_CHUNK)
    v = h_refs[r][rows, :]
    if r >= n_sb - n_ctx_refs:
        v = jnp.where(is_ctx, h_refs[n_sb + r - (n_sb - n_ctx_refs)][rows, :], v)
    return v


def _split_h_specs(x, ctx, tm, sub, idx):
    batch, seq, d = x.shape
    ctx_len = ctx.shape[1]
    assert tm % sub == 0 and seq % sub == 0 and ctx_len % sub == 0 and (seq + ctx_len) % tm == 0
    n_sb, last = tm // sub, seq // sub - 1
    xv = x.reshape(batch, seq // sub, sub, d)
    cv = ctx.reshape(batch, ctx_len // sub, sub, d)
    specs = [pl.BlockSpec((None, None, sub, d),
                          lambda *g, r=r: (idx(*g)[0], jnp.minimum(idx(*g)[1] * n_sb + r, last), 0, 0))
             for r in range(n_sb)]
    specs += [pl.BlockSpec((None, None, sub, d), lambda *g, q=q: (idx(*g)[0], q, 0, 0))
              for q in range(ctx_len // sub)]
    return [xv] * n_sb + [cv] * (ctx_len // sub), specs, ctx_len // sub


def _prologue(h_refs, n_ctx_refs, mod_b_ref, mod_c_ref, preg_ref, xn_ref, slot, row0, seq, pieces):
    preg = preg_ref[...]
    for c in pieces:
        rows = pl.ds(c * ROW_CHUNK, ROW_CHUNK)
        is_ctx = (row0 + c * ROW_CHUNK) >= seq
        shift = _mod_rows(mod_b_ref, mod_c_ref, is_ctx, 0)
        scale = _mod_rows(mod_b_ref, mod_c_ref, is_ctx, 1)
        y = _rms(_h_chunk(h_refs, n_ctx_refs, c, is_ctx)) * preg
        xn_ref[slot, rows, :] = (y * (1.0 + scale) + shift).astype(BF16)


def _proj_kernel(*refs, kinds, out_names, n_h, n_ctx_refs, n_tabs, seq, n_tiles, tiles_per_batch,
                 rope_half):
    h_refs, refs = refs[:n_h], refs[n_h:]
    mod_b_ref, mod_c_ref, preg_ref, w_ref = refs[:4]
    tabs = refs[4:4 + n_tabs]
    ng0_ref, ng1_ref = refs[4 + n_tabs:6 + n_tabs]
    outs = dict(zip(out_names, refs[6 + n_tabs:]))
    xn_ref = refs[-1]
    i = pl.program_id(0)
    j = pl.program_id(1)
    tm = xn_ref.shape[1]
    n_steps = len(kinds)
    n_pieces = tm // ROW_CHUNK
    row0 = (jnp.minimum(i, n_tiles - 1) % tiles_per_batch) * tm
    prologue = functools.partial(_prologue, h_refs, n_ctx_refs, mod_b_ref, mod_c_ref, preg_ref,
                                 xn_ref, i % 2, row0, seq)

    @pl.when((i == 0) & (j == 0))
    def _():
        prologue(range(n_pieces))

    def heads_t(a):
        at = a.T
        return [at[c * LANES:(c + 1) * LANES] for c in range(at.shape[0] // LANES)]

    def head_rms(xh):
        return lax.rsqrt(jnp.mean(xh * xh, axis=0, keepdims=True) + NORM_EPS)

    def tile(kind):
        a = jnp.dot(xn_ref[(i + 1) % 2], w_ref[...], preferred_element_type=F32)
        if kind == "g":
            outs["g"][...] = a
        elif kind == "v":
            outs["vt"][...] = a.T.astype(BF16)
        elif kind in ("da_q", "gq_q"):
            cos_t, sin_t = tabs[0][...], tabs[1][...]
            for c, xh in enumerate(heads_t(a)):
                y = _rope_t(xh, cos_t, sin_t, rope_half)
                if kind == "gq_q":
                    y = y * head_rms(xh)
                outs["qt"][c * LANES:(c + 1) * LANES, :] = y.astype(BF16)
        elif kind in ("da_k", "gq_kv"):
            cos_t, sin_t = tabs[2][...], tabs[3][...]
            k_cols = outs["o"].shape[1]
            for c, xh in enumerate(heads_t(a[:, :k_cols])):
                y = _rope_t(xh, cos_t, sin_t, rope_half)
                if kind == "gq_kv":
                    y = y * head_rms(xh)
                outs["o"][:, c * LANES:(c + 1) * LANES] = y.T.astype(BF16)
            if kind == "gq_kv":
                outs["vt"][...] = a[:, k_cols:].T.astype(BF16)
        elif kind == "ml_qa":
            outs["o"][...] = (_rms(a) * ng0_ref[...]).astype(BF16)
        elif kind == "ml_kva":
            outs["o"][...] = (_rms(a) * ng1_ref[...]).astype(BF16)
        elif kind == "ml_kr":
            outs["o"][...] = a.astype(BF16)
            outs["o"][:, :LANES] = _rope(a[:, :LANES], tabs[0][...], tabs[1][...], tabs[2][...],
                                         rope_half).astype(BF16)
        else:
            raise ValueError(kind)

    def step(jv):
        tile(kinds[jv])
        prologue([p for p in range(n_pieces) if n_steps - 1 - p % n_steps == jv])

    for jv in range(n_steps):
        pl.when((i > 0) & (j == jv))(functools.partial(step, jv))


def _proj_call(h, mod3, pre_g, w, tabs, ng0, ng1, *, kinds, tn, o_cols, v_rows, tabs_transposed,
               batch, seq, ctx_len, rope_half):
    d = D_MODEL
    m = batch * (seq + ctx_len)
    n = w.shape[1]
    t = seq + ctx_len
    tm = PROJ_TM
    tpb = t // tm
    n_tiles = m // tm
    assert t % tm == 0 and n % tn == 0 and len(kinds) == n // tn

    def nxt(i):
        return jnp.minimum(i, n_tiles - 1)

    def cur(i):
        return jnp.maximum(i - 1, 0)

    def tile_idx(name):
        js = [jv for jv, k in enumerate(kinds) if name in _KIND_OUTS[k]]
        assert js == list(range(js[0], js[-1] + 1))
        return lambda i, j: jnp.where(i == 0, 0, jnp.clip(j - js[0], 0, len(js) - 1)), len(js)

    out_names, out_specs, out_shape = [], [], []
    for name in ("qt", "o", "vt", "g"):
        if not any(name in _KIND_OUTS[k] for k in kinds):
            continue
        idx, cnt = tile_idx(name)
        out_names.append(name)
        if name in ("qt", "vt"):
            rows = tn if name == "qt" else v_rows
            out_specs.append(pl.BlockSpec((rows, tm), lambda i, j, idx=idx: (idx(i, j), cur(i))))
            out_shape.append(jax.ShapeDtypeStruct((cnt * rows, m), BF16))
        else:
            cols, dt = (o_cols, BF16) if name == "o" else (tn, F32)
            out_specs.append(pl.BlockSpec((tm, cols), lambda i, j, idx=idx: (cur(i), idx(i, j))))
            out_shape.append(jax.ShapeDtypeStruct((m, cnt * cols), dt))
    assert out_shape[-1].shape == (m, d)

    if tabs_transposed:
        tab_spec = pl.BlockSpec((LANES, tm), lambda i, j: (0, cur(i) % tpb))
    else:
        tab_spec = pl.BlockSpec((tm, LANES), lambda i, j: (cur(i) % tpb, 0))
    if isinstance(h, tuple):
        h_args, h_specs, n_ctx_refs = _split_h_specs(
            *h, tm, PROJ_SUB, lambda i, j: (nxt(i) // tpb, nxt(i) % tpb))
    else:
        h_args, h_specs, n_ctx_refs = [h], [pl.BlockSpec((tm, d), lambda i, j: (nxt(i), 0))], 0
    kern = functools.partial(_proj_kernel, kinds=tuple(kinds), out_names=tuple(out_names),
                             n_h=len(h_args), n_ctx_refs=n_ctx_refs, n_tabs=len(tabs), seq=seq,
                             n_tiles=n_tiles, tiles_per_batch=tpb, rope_half=rope_half)
    return pl.pallas_call(
        kern,
        grid=(n_tiles + 1, n // tn),
        in_specs=h_specs + [
            pl.BlockSpec((None, 1, 3 * d), lambda i, j: (nxt(i) // tpb, 0, 0)),
            pl.BlockSpec((None, 1, 3 * d), lambda i, j: (batch, 0, 0)),
            pl.BlockSpec((1, d), lambda i, j: (0, 0)),
            pl.BlockSpec((d, tn), lambda i, j: (0, jnp.where(i == 0, 0, j))),
        ] + [tab_spec] * len(tabs) + [
            pl.BlockSpec(ng0.shape, lambda i, j: (0, 0)),
            pl.BlockSpec(ng1.shape, lambda i, j: (0, 0)),
        ],
        out_specs=out_specs,
        out_shape=out_shape,
        scratch_shapes=[pltpu.VMEM((2, tm, d), BF16)],
        compiler_params=pltpu.CompilerParams(dimension_semantics=("arbitrary", "arbitrary"),
                                             vmem_limit_bytes=VMEM_LIMIT),
        name="proj",
    )(*h_args, mod3, mod3, pre_g.reshape(1, d), w, *tabs, ng0, ng1)


def _mla_b_kernel(qa_ref, kva_ref, kr_ref, wq_ref, wkn_ref, wv_ref, cos_t_ref, sin_t_ref,
                  qt_ref, k_ref, vt_ref, *, q_scale):
    qt = jnp.dot(qa_ref[...], wq_ref[...], preferred_element_type=F32).T
    for c in range(qt.shape[0] // LANES):
        x = qt[c * LANES:(c + 1) * LANES]
        if c % 2 == 1:
            x = _rope_t(x, cos_t_ref[...], sin_t_ref[...], ML_ROPE_DIM // 4)
        else:
            x = x * q_scale
        qt_ref[c * LANES:(c + 1) * LANES, :] = x.astype(BF16)
    kva = kva_ref[...]
    kn = jnp.dot(kva, wkn_ref[...], preferred_element_type=F32)
    kr = kr_ref[...]
    for hh in range(kn.shape[1] // LANES):
        k_ref[:, 2 * hh * LANES:(2 * hh + 1) * LANES] = kn[:, hh * LANES:(hh + 1) * LANES].astype(BF16)
        k_ref[:, (2 * hh + 1) * LANES:(2 * hh + 2) * LANES] = kr
    vt_ref[...] = jnp.dot(kva, wv_ref[...], preferred_element_type=F32).T.astype(BF16)


def _mla_b_call(qkv, wq, wkv, tabs_t, *, seq, ctx_len, q_scale):
    m = qkv.shape[0]
    tm = MLA_B_TM
    tpb = (seq + ctx_len) // tm
    assert (seq + ctx_len) % tm == 0
    r = ML_Q_RANK
    nq = wq.shape[1]
    nkv = wkv.shape[1] // 2
    steps = 8
    tq, tk = nq // steps, nkv // steps
    kr_col = (ML_Q_RANK + ML_KV_RANK) // LANES
    tab_spec = pl.BlockSpec((LANES, tm), lambda i, j: (0, i % tpb))
    return pl.pallas_call(
        functools.partial(_mla_b_kernel, q_scale=q_scale),
        grid=(m // tm, steps),
        in_specs=[
            pl.BlockSpec((tm, r), lambda i, j: (i, 0)),
            pl.BlockSpec((tm, r), lambda i, j: (i, 1)),
            pl.BlockSpec((tm, LANES), lambda i, j: (i, kr_col)),
            pl.BlockSpec((r, tq), lambda i, j: (0, j)),
            pl.BlockSpec((r, tk), lambda i, j: (0, j)),
            pl.BlockSpec((r, tk), lambda i, j: (0, steps + j)),
            tab_spec, tab_spec,
        ],
        out_specs=[
            pl.BlockSpec((tq, tm), lambda i, j: (j, i)),
            pl.BlockSpec((tm, 2 * tk), lambda i, j: (i, j)),
            pl.BlockSpec((tk, tm), lambda i, j: (j, i)),
        ],
        out_shape=[jax.ShapeDtypeStruct((nq, m), BF16),
                   jax.ShapeDtypeStruct((m, 2 * nkv), BF16),
                   jax.ShapeDtypeStruct((nkv, m), BF16)],
        compiler_params=pltpu.CompilerParams(dimension_semantics=("arbitrary", "arbitrary"),
                                             vmem_limit_bytes=VMEM_LIMIT),
        name="mla_b",
    )(qkv, qkv, qkv, wq, wkv, wkv, *tabs_t)


def _sublane_groups(x):
    return x.reshape(x.shape[0] // SUBLANES, SUBLANES, x.shape[1])


def _run_chains(chains, s_scr, e_scr):
    n = len(chains)
    shape, m, l = {}, {}, {}
    for step in range(n + 2):
        if step < n:
            qt, k, vt, _ = chains[step]
            qv = qt()
            nq, nk = qv.shape[1], vt().shape[1]
            shape[step] = (nk, nq)
            kc = min(KEY_CHUNK, nk)
            m8 = None
            for c0 in range(0, nk, kc):
                st = jnp.dot(k(c0, kc), qv, preferred_element_type=F32)
                s_scr[step % 2, c0:c0 + kc, :nq] = st
                cm = jnp.max(_sublane_groups(st), axis=0)
                m8 = cm if m8 is None else jnp.maximum(m8, cm)
            m[step] = jnp.max(m8, axis=0, keepdims=True)
        i = step - 1
        if 0 <= i < n:
            nk, nq = shape[i]
            mi = m.pop(i)
            kc = min(KEY_CHUNK, nk)
            l8 = None
            for c0 in range(0, nk, kc):
                p = jnp.exp2(s_scr[i % 2, c0:c0 + kc, :nq] - mi)
                ps = jnp.sum(_sublane_groups(p), axis=0)
                l8 = ps if l8 is None else l8 + ps
                e_scr[i % 2, c0:c0 + kc, :nq] = p.astype(BF16)
            l[i] = jnp.sum(l8, axis=0, keepdims=True)
        i = step - 2
        if 0 <= i < n:
            nk, nq = shape[i]
            _, _, vt, done = chains[i]
            done(jnp.dot(vt(), e_scr[i % 2, :nk, :nq], preferred_element_type=F32) / l.pop(i))


def _row_blocks(rows_q, seq):
    assert seq % MXU_N == 0
    blocks = [(r0, MXU_N, 0) for r0 in range(0, seq, MXU_N)]
    if rows_q > seq:
        blocks.append((seq, rows_q - seq, seq))
    return blocks


def _diff_attn_kernel(lq1_ref, lk1_ref, lq2_ref, lk2_ref, subg_ref, qt_ref, k_ref, vt_ref, g_ref,
                      o_ref, s_scr, e_scr, *, seq, lambda_init):
    lam = (jnp.exp(jnp.sum(lq1_ref[...] * lk1_ref[...], axis=-1, keepdims=True))
           - jnp.exp(jnp.sum(lq2_ref[...] * lk2_ref[...], axis=-1, keepdims=True))
           + lambda_init)
    chains = []
    for r0, nr, key0 in _row_blocks(o_ref.shape[0], seq):
        rows = slice(r0, r0 + nr)
        for hh in range(qt_ref.shape[0] // LANES):
            cols = slice(hh * LANES, (hh + 1) * LANES)
            parts = []

            def q_sub(sub, rows=rows, cols=cols):
                qf = qt_ref[cols, rows].astype(F32)
                row = lax.broadcasted_iota(jnp.int32, qf.shape, 0)
                return jnp.where((row >= DA_HEAD_DIM) == bool(sub), qf, 0.0).astype(BF16)

            def done(ot, rows=rows, cols=cols, parts=parts):
                parts.append(ot)
                if len(parts) == 2:
                    o = (parts[0] - lam * parts[1]).T
                    o = _rms(o) * subg_ref[...] * (1.0 - lambda_init)
                    o_ref[rows, cols] = (o * _silu(g_ref[rows, cols])).astype(BF16)

            for sub in range(2):
                chains.append((functools.partial(q_sub, sub),
                               lambda c0, kc, key0=key0, cols=cols: k_ref[key0 + c0:key0 + c0 + kc,
                                                                          cols],
                               lambda key0=key0, cols=cols: vt_ref[cols, key0:], done))
    _run_chains(chains, s_scr, e_scr)


def _plain_attn_kernel(qt_ref, k_ref, vt_ref, g_ref, o_ref, s_scr, e_scr, *, seq, dq, shared_kv):
    n_heads = qt_ref.shape[0] // dq
    chains = []
    for r0, nr, key0 in _row_blocks(o_ref.shape[0], seq):
        rows = slice(r0, r0 + nr)
        for hh in range(n_heads):
            kh = 0 if shared_kv else hh
            cols = slice(hh * LANES, (hh + 1) * LANES)

            def done(ot, rows=rows, cols=cols):
                o_ref[rows, cols] = (ot.T * _silu(g_ref[rows, cols])).astype(BF16)

            chains.append((lambda rows=rows, hh=hh: qt_ref[hh * dq:(hh + 1) * dq, rows],
                           lambda c0, kc, key0=key0, kh=kh: k_ref[key0 + c0:key0 + c0 + kc,
                                                                  kh * dq:(kh + 1) * dq],
                           lambda key0=key0, kh=kh: vt_ref[kh * LANES:(kh + 1) * LANES, key0:],
                           done))
    _run_chains(chains, s_scr, e_scr)


def _attn_call(kind, qt, k3, vt, g3, small, *, seq, need_ctx, lambda_init=None):
    batch, t, _ = k3.shape
    rows = t if need_ctx else seq
    heads = D_MODEL // LANES
    if kind == "diff":
        hps, kv_heads, dq, k_col0 = DA_HEADS_PER_STEP, DA_HEADS_PER_STEP, LANES, 0
        kern = functools.partial(_diff_attn_kernel, seq=seq, lambda_init=lambda_init)
    elif kind == "gqa":
        hps, kv_heads, dq, k_col0 = GQ_HEADS // GQ_KV_HEADS, 1, LANES, 0
        kern = functools.partial(_plain_attn_kernel, seq=seq, dq=dq, shared_kv=True)
    elif kind == "mla":
        hps, kv_heads, dq, k_col0 = MLA_HEADS_PER_STEP, MLA_HEADS_PER_STEP, 2 * LANES, 0
        kern = functools.partial(_plain_attn_kernel, seq=seq, dq=dq, shared_kv=False)
    else:
        raise ValueError(kind)
    kw = kv_heads * dq
    in_specs = [pl.BlockSpec(a.shape, lambda b, s: (0,) * a.ndim) for a in small] + [
        pl.BlockSpec((hps * dq, t), lambda b, s: (s, b)),
        pl.BlockSpec((None, t, kw), lambda b, s: (b, 0, k_col0 + s)),
        pl.BlockSpec((kv_heads * LANES, t), lambda b, s: (s, b)),
        pl.BlockSpec((None, rows, hps * LANES), lambda b, s: (b, 0, s)),
    ]
    return pl.pallas_call(
        kern,
        grid=(batch, heads // hps),
        in_specs=in_specs,
        out_specs=pl.BlockSpec((None, rows, hps * LANES), lambda b, s: (b, 0, s)),
        out_shape=jax.ShapeDtypeStruct((batch, rows, D_MODEL), BF16),
        scratch_shapes=[pltpu.VMEM((2, t, MXU_N), F32), pltpu.VMEM((2, t, MXU_N), BF16)],
        compiler_params=pltpu.CompilerParams(dimension_semantics=("arbitrary", "arbitrary"),
                                             vmem_limit_bytes=VMEM_LIMIT),
        name="attn_" + kind,
    )(*small, qt, k3, vt, g3)


def _finish_kernel(*refs, seq, n_h, n_ctx_refs, n_cast):
    h_refs, refs = refs[:n_h], refs[n_h:]
    a_ref, mod_b_ref, mod_c_ref, postg_ref, w_ref = refs[:5]
    n_in = 5 + (3 + n_cast if n_cast else 0)
    o_ref = refs[n_in]
    ti = pl.program_id(1)
    tm = a_ref.shape[0]
    y = jnp.dot(a_ref[...], w_ref[...], preferred_element_type=F32)
    postg = postg_ref[...]
    for c in range(tm // ROW_CHUNK):
        rows = slice(c * ROW_CHUNK, (c + 1) * ROW_CHUNK)
        is_ctx = (ti * tm + c * ROW_CHUNK) >= seq
        gate = _mod_rows(mod_b_ref, mod_c_ref, is_ctx, 2)
        o_ref[rows, :] = (_h_chunk(h_refs, n_ctx_refs, c, is_ctx)
                          + gate * (_rms(y[rows, :]) * postg))
    if n_cast:
        c_ref, adaw_ref, adab_ref = refs[5:8]
        _ada_kernel(c_ref, adaw_ref, adab_ref, refs[n_in + 1])
        for src, dst in zip(refs[8:n_in], refs[n_in + 2:]):
            dst[...] = src[...].astype(BF16)


def _finish_call(a3, h, mod3, post_g, w, *, seq, nxt=None):
    batch, rows_out, d = a3.shape
    t = sum(v.shape[1] for v in h) if isinstance(h, tuple) else h.shape[1]
    tm = FINISH_TM if rows_out == t else FINISH_TM_LAST
    assert rows_out % tm == 0
    n_i = rows_out // tm
    blk = pl.BlockSpec((None, tm, d), lambda b, i: (b, i, 0))
    if isinstance(h, tuple):
        h_args, h_specs, n_ctx_refs = _split_h_specs(*h, tm, FINISH_SUB, lambda b, i: (b, i))
    else:
        h_args, h_specs, n_ctx_refs = [h], [blk], 0
    in_specs = h_specs + [
        blk,
        pl.BlockSpec((None, 1, 3 * d), lambda b, i: (b, 0, 0)),
        pl.BlockSpec((None, 1, 3 * d), lambda b, i: (batch, 0, 0)),
        pl.BlockSpec((1, d), lambda b, i: (0, 0)),
        pl.BlockSpec((d, d), lambda b, i: (0, 0)),
    ]
    args = h_args + [a3, mod3, mod3, post_g.reshape(1, d), w]
    out_specs = [blk]
    out_shape = [jax.ShapeDtypeStruct((batch, rows_out, d), F32)]
    names = []
    if nxt is not None:
        cc, ada_w, ada_b, weights = nxt
        n_mod = ada_w.shape[1]
        assert n_mod == batch * n_i * LANES
        flat = lambda b, i: (0, b * n_i + i)
        in_specs += [pl.BlockSpec((MOD_ROWS, d), lambda b, i: (0, 0)),
                     pl.BlockSpec((d, LANES), flat), pl.BlockSpec((1, LANES), flat)]
        args += [cc, ada_w, ada_b.reshape(1, n_mod)]
        out_specs.append(pl.BlockSpec((MOD_ROWS, LANES), flat))
        out_shape.append(jax.ShapeDtypeStruct((MOD_ROWS, n_mod), F32))
        for name, wf in weights.items():
            r, c = wf.shape
            assert r % (batch * 2 * SUBLANES) == 0 and c % (4 * LANES) == 0 and n_i >= 4
            spec = pl.BlockSpec((r // batch, c // 4), lambda b, i: (b, jnp.minimum(i, 3)))
            names.append(name)
            in_specs.append(spec)
            args.append(wf)
            out_specs.append(spec)
            out_shape.append(jax.ShapeDtypeStruct((r, c), BF16))
    outs = pl.pallas_call(
        functools.partial(_finish_kernel, seq=seq, n_h=len(h_args), n_ctx_refs=n_ctx_refs,
                          n_cast=len(names)),
        grid=(batch, n_i),
        in_specs=in_specs,
        out_specs=out_specs,
        out_shape=out_shape,
        compiler_params=pltpu.CompilerParams(dimension_semantics=("arbitrary", "arbitrary"),
                                             vmem_limit_bytes=VMEM_LIMIT),
        name="finish",
    )(*args)
    if nxt is None:
        return outs[0], None, None
    return outs[0], outs[1], dict(zip(names, outs[2:]))


def _lambda_init(layer):
    return 0.8 - 0.6 * math.exp(-0.3 * layer)


def _prep_weights(kind, p):
    if kind != "mla":
        return dict(w_in=p["w_in"], w_out=p["w_out"]), {}
    w_in = p["w_in"].astype(BF16)
    d = w_in.shape[0]
    r2 = ML_Q_RANK + ML_KV_RANK
    ready = {}
    ready["w_in"] = jnp.concatenate(
        [w_in[:, :r2 + ML_ROPE_DIM], jnp.zeros((d, ML_Q_RANK - ML_ROPE_DIM), BF16),
         w_in[:, r2 + ML_ROPE_DIM:]], axis=1)
    qd = ML_NOPE_DIM + ML_ROPE_DIM
    wq = p["w_q_b"].astype(BF16).reshape(ML_Q_RANK, ML_HEADS, qd)
    wq = jnp.pad(wq, ((0, 0), (0, 0), (0, 2 * LANES - qd)))
    ready["wq"] = wq.reshape(ML_Q_RANK, ML_HEADS * 2 * LANES)
    wkv = p["w_kv_b"].astype(BF16).reshape(ML_KV_RANK, ML_HEADS, ML_NOPE_DIM + ML_V_DIM)
    ready["wkv"] = jnp.concatenate([wkv[:, :, :ML_NOPE_DIM].reshape(ML_KV_RANK, -1),
                                    wkv[:, :, ML_NOPE_DIM:].reshape(ML_KV_RANK, -1)], axis=1)
    return dict(w_out=p["w_out"]), ready


def _layer(kind, layer, h3, mod, wts, p, nxt, *, seq, ctx_len):
    d = D_MODEL
    t = seq + ctx_len
    batch = h3[0].shape[0] if isinstance(h3, tuple) else h3.shape[0]
    need_ctx = nxt is not None
    mod3 = mod.reshape(MOD_ROWS, 1, 3 * d)
    h2 = h3 if isinstance(h3, tuple) else h3.reshape(batch * t, d)
    one = jnp.ones((1, LANES), F32)
    common = dict(batch=batch, seq=seq, ctx_len=ctx_len)
    tables_t = functools.partial(_rope_tables_t, seq, ctx_len)

    def b3(a):
        return a.reshape(batch, t, a.shape[-1])

    if kind == "diff":
        tabs = (tables_t(DA_HEAD_DIM, None, DA_HEAD_DIM ** -0.5 * LOG2_E)
                + tables_t(DA_HEAD_DIM, None, 1.0))
        qt, k, vt, g = _proj_call(h2, mod3, p["pre_g"], wts["w_in"], tabs, one, one,
                                  kinds=["da_q"] * 2 + ["da_k"] * 2 + ["v"] * 2 + ["g"] * 2,
                                  tn=1024, o_cols=1024, v_rows=1024, tabs_transposed=True,
                                  rope_half=DA_HEAD_DIM // 4, **common)
        small = [p[n].reshape(1, DA_HEAD_DIM) for n in ("lam_q1", "lam_k1", "lam_q2", "lam_k2")]
        small.append(p["subln_g"].reshape(1, 2 * DA_HEAD_DIM))
        a3 = _attn_call("diff", qt, b3(k), vt, b3(g), small, seq=seq, need_ctx=need_ctx,
                        lambda_init=_lambda_init(layer))
    elif kind == "gqa":
        tabs = (tables_t(GQ_HEAD_DIM, p["q_norm_g"], GQ_HEAD_DIM ** -0.5 * LOG2_E)
                + tables_t(GQ_HEAD_DIM, p["k_norm_g"], 1.0))
        qt, k, vt, g = _proj_call(h2, mod3, p["pre_g"], wts["w_in"], tabs, one, one,
                                  kinds=["gq_q"] * 2 + ["gq_kv"] + ["g"] * 2,
                                  tn=1024, o_cols=GQ_KV_W, v_rows=GQ_KV_W, tabs_transposed=True,
                                  rope_half=GQ_HEAD_DIM // 4, **common)
        a3 = _attn_call("gqa", qt, b3(k), vt, b3(g), [], seq=seq, need_ctx=need_ctx)
    elif kind == "mla":
        tn = ML_Q_RANK
        qkv, g = _proj_call(h2, mod3, p["pre_g"], wts["w_in"],
                            _rope_tables(seq, ctx_len, ML_ROPE_DIM),
                            p["q_a_norm_g"].reshape(1, ML_Q_RANK),
                            p["kv_a_norm_g"].reshape(1, ML_KV_RANK),
                            kinds=["ml_qa", "ml_kva", "ml_kr"] + ["g"] * (d // tn),
                            tn=tn, o_cols=tn, v_rows=0, tabs_transposed=False,
                            rope_half=ML_ROPE_DIM // 4, **common)
        qd = ML_NOPE_DIM + ML_ROPE_DIM
        q_scale = qd ** -0.5 * LOG2_E
        qt, k, vt = _mla_b_call(qkv, wts["wq"], wts["wkv"], tables_t(ML_ROPE_DIM, None, q_scale),
                                seq=seq, ctx_len=ctx_len, q_scale=q_scale)
        a3 = _attn_call("mla", qt, b3(k), vt, b3(g), [], seq=seq, need_ctx=need_ctx)
    else:
        raise ValueError(kind)
    return _finish_call(a3, h3, mod3, p["post_g"], wts["w_out"], seq=seq, nxt=nxt)


def kernel(x, c, ctx, c_ctx, l0_ada_w, l0_ada_b, l0_pre_g, l0_post_g, l0_w_in, l0_lam_q1, l0_lam_k1, l0_lam_q2, l0_lam_k2, l0_subln_g, l0_w_out, l1_ada_w, l1_ada_b, l1_pre_g, l1_post_g, l1_w_in, l1_q_norm_g, l1_k_norm_g, l1_w_out, l2_ada_w, l2_ada_b, l2_pre_g, l2_post_g, l2_w_in, l2_q_a_norm_g, l2_w_q_b, l2_kv_a_norm_g, l2_w_kv_b, l2_w_out, l3_ada_w, l3_ada_b, l3_pre_g, l3_post_g, l3_w_in, l3_lam_q1, l3_lam_k1, l3_lam_q2, l3_lam_k2, l3_subln_g, l3_w_out):
    batch, seq, d = x.shape
    ctx_len = ctx.shape[1]
    assert d == D_MODEL and seq % GRID_W == 0 and batch + 1 <= MOD_ROWS
    diff_names = ("ada_w", "ada_b", "pre_g", "post_g", "w_in", "lam_q1", "lam_k1", "lam_q2",
                  "lam_k2", "subln_g", "w_out")
    layers = [
        ("diff", dict(zip(diff_names, (l0_ada_w, l0_ada_b, l0_pre_g, l0_post_g, l0_w_in,
                                       l0_lam_q1, l0_lam_k1, l0_lam_q2, l0_lam_k2, l0_subln_g,
                                       l0_w_out)))),
        ("gqa", dict(ada_w=l1_ada_w, ada_b=l1_ada_b, pre_g=l1_pre_g, post_g=l1_post_g,
                     w_in=l1_w_in, q_norm_g=l1_q_norm_g, k_norm_g=l1_k_norm_g, w_out=l1_w_out)),
        ("mla", dict(ada_w=l2_ada_w, ada_b=l2_ada_b, pre_g=l2_pre_g, post_g=l2_post_g,
                     w_in=l2_w_in, q_a_norm_g=l2_q_a_norm_g, w_q_b=l2_w_q_b,
                     kv_a_norm_g=l2_kv_a_norm_g, w_kv_b=l2_w_kv_b, w_out=l2_w_out)),
        ("diff", dict(zip(diff_names, (l3_ada_w, l3_ada_b, l3_pre_g, l3_post_g, l3_w_in,
                                       l3_lam_q1, l3_lam_k1, l3_lam_q2, l3_lam_k2, l3_subln_g,
                                       l3_w_out)))),
    ]
    cc = jnp.zeros((MOD_ROWS, d), F32).at[:batch].set(c).at[batch].set(c_ctx)
    h3 = (x, ctx)
    kind0, p0 = layers[0]
    mod = _ada_call(cc, p0["ada_w"], p0["ada_b"])
    to_cast, ready = _prep_weights(kind0, p0)
    wts = dict(ready, **{name: w.astype(BF16) for name, w in to_cast.items()})
    for layer, (kind, p) in enumerate(layers):
        nxt, ready = None, {}
        if layer + 1 < len(layers):
            kind_n, p_n = layers[layer + 1]
            to_cast, ready = _prep_weights(kind_n, p_n)
            nxt = (cc, p_n["ada_w"], p_n["ada_b"], to_cast)
        h3, mod, cast = _layer(kind, layer, h3, mod, wts, p, nxt, seq=seq, ctx_len=ctx_len)
        wts = dict(ready, **(cast or {}))
    return h3
```
